```python
import math
import jax
import jax.numpy as jnp
from jax import lax
import numpy as np

D_MODEL = 1024
BATCH = 8
SEQ = 4096
DEPTH = 2

CHUNK = 64
QBLK = 128

POOL_WINDOWS = (2, 4, 8, 16)
POOL_GROUP = D_MODEL // 8
POOL_WIDTH = POOL_GROUP * len(POOL_WINDOWS)

DSA_HEADS = 8
DSA_HEAD_DIM = D_MODEL // 16
DSA_WIDTH = DSA_HEADS * DSA_HEAD_DIM
KV_RANK = D_MODEL // 8
IDX_HEADS = 4
IDX_DIM = D_MODEL // 16
TOPK_MAX = 256
MIX_WIDTH = POOL_WIDTH + DSA_WIDTH
EVEN_SPLITS = (POOL_WIDTH,
               POOL_WIDTH + DSA_WIDTH,
               POOL_WIDTH + DSA_WIDTH + KV_RANK,
               POOL_WIDTH + DSA_WIDTH + KV_RANK + IDX_HEADS * IDX_DIM,
               POOL_WIDTH + DSA_WIDTH + KV_RANK + IDX_HEADS * IDX_DIM + IDX_DIM)
EVEN_IN = EVEN_SPLITS[-1] + IDX_HEADS

GDN_HEADS = 8
GDN_HEAD_DIM = D_MODEL // GDN_HEADS
GDN_WIDTH = GDN_HEADS * GDN_HEAD_DIM
SHORT_CONV = 4
ODD_SPLITS = (3 * GDN_WIDTH, 4 * GDN_WIDTH, 4 * GDN_WIDTH + GDN_HEADS)
ODD_IN = 4 * GDN_WIDTH + 2 * GDN_HEADS

D_FF = (8 * D_MODEL // 3) // 128 * 128
FFN_CONV = 3

DEEPNORM_ALPHA = (2 * DEPTH) ** 0.25
DEEPNORM_BETA = (8 * DEPTH) ** -0.25
LN_EPS = 1e-5
RMS_EPS = 1e-6
N_EVEN = (DEPTH + 1) // 2
N_ODD = DEPTH // 2

kernel_name = 'hybrid_pool_dsa_gdn_convffn_trunk'


def layer_norm(x, g, b):
    xf = x.astype(jnp.float32)
    mu = xf.mean(-1, keepdims=True)
    var = jnp.square(xf - mu).mean(-1, keepdims=True)
    return ((xf - mu) * lax.rsqrt(var + LN_EPS) * g + b).astype(x.dtype)


def rms_norm(x, g):
    xf = x.astype(jnp.float32)
    return (xf * lax.rsqrt(jnp.mean(xf * xf, -1, keepdims=True) + RMS_EPS) * g).astype(x.dtype)


def l2_norm(x):
    xf = x.astype(jnp.float32)
    return xf * lax.rsqrt(jnp.sum(xf * xf, -1, keepdims=True) + RMS_EPS)


def causal_dwconv(x, w):
    k = w.shape[0]
    return lax.conv_general_dilated(
        x, w[:, None, :], window_strides=(1,), padding=[(k - 1, 0)],
        dimension_numbers=('NWC', 'WIO', 'NWC'), feature_group_count=x.shape[-1])


def adaln_post_norm(x, c, mod_w, mod_b, ln_g, ln_b, sublayer):
    mod = jax.nn.silu(c) @ mod_w + mod_b
    shift, scale, gate = jnp.split(mod[:, None, :], 3, axis=-1)
    y = sublayer(x * (1.0 + scale) + shift)
    return layer_norm(DEEPNORM_ALPHA * x + gate * y, ln_g, ln_b)


def pool_mixer(u, w_pool, scale):
    b, s, _ = u.shape
    ug = u.reshape(b, s, len(POOL_WINDOWS), POOL_GROUP)
    cs = jnp.pad(jnp.cumsum(ug.astype(jnp.float32), axis=1), ((0, 0), (1, 0), (0, 0), (0, 0)))
    t = jnp.arange(1, s + 1)
    means = []
    for gi, win in enumerate(POOL_WINDOWS):
        lo = jnp.maximum(t - win, 0)
        cnt = (t - lo).astype(jnp.float32)
        means.append((cs[:, 1:, gi] - cs[:, lo, gi]) / cnt[None, :, None])
    pooled = jnp.stack(means, axis=2).astype(u.dtype) - ug
    y = jnp.einsum('bsgc,gcd->bsgd', pooled, w_pool)
    return y.reshape(b, s, POOL_WIDTH) * scale


def dsa_attention(q, ckv, q_idx, k_idx, w_idx, w_uk, w_uv):
    b, s, h, dh = q.shape
    topk = min(TOPK_MAX, s // 4)
    nblk = s // QBLK
    q_lat = jnp.einsum('bshd,hrd->bshr', q, w_uk)
    w_idx = w_idx * (IDX_HEADS ** -0.5 * IDX_DIM ** -0.5)
    key_chunk = jnp.arange(s) // CHUNK

    def blocks(t):
        return jnp.moveaxis(t.reshape((b, nblk, QBLK) + t.shape[2:]), 1, 0)

    def attend(args):
        ql, qi, wi, pos = args
        qchunk = pos // CHUNK
        rel = jax.nn.relu(jnp.einsum('bthd,bsd->bths', qi, k_idx).astype(jnp.float32))
        score = jnp.einsum('bths,bth->bts', rel, wi.astype(jnp.float32))
        adm = key_chunk[None, :] <= qchunk[:, None]
        score = jnp.where(adm[None], score, -jnp.inf)
        _, sel = lax.top_k(score, topk)
        kv_sel = jax.vmap(lambda cb, ib: cb[ib])(ckv, sel)
        valid = (sel // CHUNK) <= qchunk[None, :, None]
        logits = jnp.einsum('bthr,btkr->bthk', ql, kv_sel).astype(jnp.float32) * dh ** -0.5
        logits = jnp.where(valid[:, :, None, :], logits, -jnp.inf)
        p = jax.nn.softmax(logits, axis=-1).astype(kv_sel.dtype)
        return jnp.einsum('bthk,btkr->bthr', p, kv_sel)

    qpos = jnp.arange(s).reshape(nblk, QBLK)
    o = lax.map(attend, (blocks(q_lat), blocks(q_idx), blocks(w_idx), qpos))
    o = jnp.moveaxis(o, 0, 1).reshape(b, s, h, KV_RANK)
    return jnp.einsum('bshr,hrd->bshd', o, w_uv)


def pool_dsa_mixer(h, w_in, pool_w, pool_scale, kv_norm, w_uk, w_uv, w_out):
    b, s, _ = h.shape
    proj = h @ w_in
    u, q, ckv, qi, ki, wi = jnp.split(proj, EVEN_SPLITS, axis=-1)
    y_pool = pool_mixer(u, pool_w, pool_scale)
    y_dsa = dsa_attention(q.reshape(b, s, DSA_HEADS, DSA_HEAD_DIM), rms_norm(ckv, kv_norm),
                          qi.reshape(b, s, IDX_HEADS, IDX_DIM), ki, wi, w_uk, w_uv)
    y = jnp.concatenate([y_pool, y_dsa.reshape(b, s, DSA_WIDTH)], axis=-1)
    return y @ w_out


def gated_delta_rule(q, k, v, beta, g):
    b, s, h, dk = q.shape
    dv = v.shape[-1]
    n = s // CHUNK

    def to_chunks(t):
        t = t.astype(jnp.float32).reshape((b, n, CHUNK, h) + t.shape[3:])
        return jnp.moveaxis(t, 3, 1)

    q, k, v, beta, g = [to_chunks(t) for t in (q, k, v, beta, g)]
    q = q * dk ** -0.5
    gc = jnp.cumsum(g, axis=-1)
    ar = jnp.arange(CHUNK)
    strict = ar[:, None] > ar[None, :]
    incl = ar[:, None] >= ar[None, :]
    diff = gc[..., :, None] - gc[..., None, :]
    dec_strict = jnp.where(strict, jnp.exp(jnp.where(strict, diff, 0.0)), 0.0)
    dec_incl = jnp.where(incl, jnp.exp(jnp.where(incl, diff, 0.0)), 0.0)
    kb = k * beta[..., None]
    a = jnp.einsum('bhncd,bhnsd->bhncs', kb, k) * dec_strict
    eye = jnp.eye(CHUNK, dtype=jnp.float32)
    t_mat = lax.linalg.triangular_solve(a + eye, jnp.broadcast_to(eye, a.shape),
                                        left_side=True, lower=True, unit_diagonal=True)
    u = t_mat @ (v * beta[..., None])
    w = t_mat @ (kb * jnp.exp(gc)[..., None])
    attn = jnp.einsum('bhncd,bhnsd->bhncs', q, k) * dec_incl
    q_dec = q * jnp.exp(gc)[..., None]
    k_dec = k * jnp.exp(gc[..., -1:] - gc)[..., None]
    chunk_decay = jnp.exp(gc[..., -1])
    xs = tuple(jnp.moveaxis(t, 2, 0) for t in (q_dec, k_dec, u, w, attn, chunk_decay))

    def step(state, inp):
        qd, kd, uc, wc, at, cd = inp
        v_new = uc - wc @ state
        o = qd @ state + at @ v_new
        state = state * cd[..., None, None] + jnp.swapaxes(kd, -1, -2) @ v_new
        return state, o

    s0 = jnp.zeros((b, h, dk, dv), jnp.float32)
    _, o = lax.scan(step, s0, xs)
    return jnp.transpose(o, (1, 0, 3, 2, 4)).reshape(b, s, h, dv)


def gdn_mixer(h, w_in, conv_w, a_log, dt_bias, out_norm, w_out):
    b, s, _ = h.shape
    proj = h @ w_in
    qkv, gate, beta_raw, a_raw = jnp.split(proj, ODD_SPLITS, axis=-1)
    qkv = jax.nn.silu(causal_dwconv(qkv, conv_w))
    q, k, v = [t.reshape(b, s, GDN_HEADS, GDN_HEAD_DIM) for t in jnp.split(qkv, 3, axis=-1)]
    beta = jax.nn.sigmoid(beta_raw.astype(jnp.float32))
    g = -jnp.exp(a_log.astype(jnp.float32)) * jax.nn.softplus(a_raw.astype(jnp.float32) + dt_bias)
    o = gated_delta_rule(l2_norm(q), l2_norm(k), v, beta, g)
    o = rms_norm(o, out_norm) * jax.nn.silu(gate.reshape(b, s, GDN_HEADS, GDN_HEAD_DIM).astype(jnp.float32))
    return o.reshape(b, s, GDN_WIDTH).astype(h.dtype) @ w_out


def conv_ffn(h, w_up, conv_w, conv_b, w_down):
    up = causal_dwconv(h @ w_up, conv_w) + conv_b
    a, v = jnp.split(up, 2, axis=-1)
    return (jax.nn.silu(a) * v) @ w_down


def setup_inputs(seed: int = 0) -> dict:
    key = jax.random.key(seed)
    keys = iter(jax.random.split(key, 64))

    def nrm(shape, std):
        return std * jax.random.normal(next(keys), shape, jnp.float32)

    d = D_MODEL
    ne, no, nl = N_EVEN, N_ODD, DEPTH
    mod_std = 0.5 * d ** -0.5
    x = nrm((BATCH, SEQ, d), 1.0)
    c = nrm((BATCH, d), 1.0)
    e_mod_w = nrm((ne, d, 3 * d), mod_std)
    e_mod_b = nrm((ne, 3 * d), 0.01)
    e_w_in = nrm((ne, d, EVEN_IN), d ** -0.5)
    e_pool_w = nrm((ne, len(POOL_WINDOWS), POOL_GROUP, POOL_GROUP), POOL_GROUP ** -0.5)
    e_pool_scale = 1.0 + nrm((ne, POOL_WIDTH), 0.1)
    e_kv_norm = 1.0 + nrm((ne, KV_RANK), 0.05)
    e_w_uk = nrm((ne, DSA_HEADS, KV_RANK, DSA_HEAD_DIM), KV_RANK ** -0.5)
    e_w_uv = nrm((ne, DSA_HEADS, KV_RANK, DSA_HEAD_DIM), KV_RANK ** -0.5)
    e_w_out = nrm((ne, MIX_WIDTH, d), DEEPNORM_BETA * MIX_WIDTH ** -0.5)
    e_ln_g = 1.0 + nrm((ne, d), 0.05)
    e_ln_b = nrm((ne, d), 0.01)
    o_mod_w = nrm((no, d, 3 * d), mod_std)
    o_mod_b = nrm((no, 3 * d), 0.01)
    o_w_in = nrm((no, d, ODD_IN), d ** -0.5)
    o_conv_w = nrm((no, SHORT_CONV, 3 * GDN_WIDTH), SHORT_CONV ** -0.5)
    o_a_log = jnp.log(1.0 + 15.0 * jax.random.uniform(next(keys), (no, GDN_HEADS), jnp.float32))
    dt = jnp.exp(jax.random.uniform(next(keys), (no, GDN_HEADS), jnp.float32,
                                    math.log(1e-3), math.log(1e-1)))
    o_dt_bias = dt + jnp.log(-jnp.expm1(-dt))
    o_out_norm = 1.0 + nrm((no, GDN_HEAD_DIM), 0.05)
    o_w_out = nrm((no, GDN_WIDTH, d), DEEPNORM_BETA * GDN_WIDTH ** -0.5)
    o_ln_g = 1.0 + nrm((no, d), 0.05)
    o_ln_b = nrm((no, d), 0.01)
    f_mod_w = nrm((nl, d, 3 * d), mod_std)
    f_mod_b = nrm((nl, 3 * d), 0.01)
    f_w_up = nrm((nl, d, 2 * D_FF), d ** -0.5)
    f_conv_w = nrm((nl, FFN_CONV, 2 * D_FF), FFN_CONV ** -0.5)
    f_conv_b = nrm((nl, 2 * D_FF), 0.01)
    f_w_down = nrm((nl, D_FF, d), DEEPNORM_BETA * D_FF ** -0.5)
    f_ln_g = 1.0 + nrm((nl, d), 0.05)
    f_ln_b = nrm((nl, d), 0.01)
    return {'x': x, 'c': c,
            'e_mod_w': e_mod_w, 'e_mod_b': e_mod_b, 'e_w_in': e_w_in, 'e_pool_w': e_pool_w,
            'e_pool_scale': e_pool_scale, 'e_kv_norm': e_kv_norm, 'e_w_uk': e_w_uk, 'e_w_uv': e_w_uv,
            'e_w_out': e_w_out, 'e_ln_g': e_ln_g, 'e_ln_b': e_ln_b,
            'o_mod_w': o_mod_w, 'o_mod_b': o_mod_b, 'o_w_in': o_w_in, 'o_conv_w': o_conv_w,
            'o_a_log': o_a_log, 'o_dt_bias': o_dt_bias, 'o_out_norm': o_out_norm, 'o_w_out': o_w_out,
            'o_ln_g': o_ln_g, 'o_ln_b': o_ln_b,
            'f_mod_w': f_mod_w, 'f_mod_b': f_mod_b, 'f_w_up': f_w_up, 'f_conv_w': f_conv_w,
            'f_conv_b': f_conv_b, 'f_w_down': f_w_down, 'f_ln_g': f_ln_g, 'f_ln_b': f_ln_b}


def reference(x, c, e_mod_w, e_mod_b, e_w_in, e_pool_w, e_pool_scale, e_kv_norm, e_w_uk, e_w_uv,
              e_w_out, e_ln_g, e_ln_b, o_mod_w, o_mod_b, o_w_in, o_conv_w, o_a_log, o_dt_bias,
              o_out_norm, o_w_out, o_ln_g, o_ln_b, f_mod_w, f_mod_b, f_w_up, f_conv_w, f_conv_b,
              f_w_down, f_ln_g, f_ln_b):
    for layer in range(DEPTH):
        i = layer // 2
        if layer % 2 == 0:
            mixer = lambda h: pool_dsa_mixer(h, e_w_in[i], e_pool_w[i], e_pool_scale[i], e_kv_norm[i],
                                             e_w_uk[i], e_w_uv[i], e_w_out[i])
            x = adaln_post_norm(x, c, e_mod_w[i], e_mod_b[i], e_ln_g[i], e_ln_b[i], mixer)
        else:
            mixer = lambda h: gdn_mixer(h, o_w_in[i], o_conv_w[i], o_a_log[i], o_dt_bias[i],
                                        o_out_norm[i], o_w_out[i])
            x = adaln_post_norm(x, c, o_mod_w[i], o_mod_b[i], o_ln_g[i], o_ln_b[i], mixer)
        ffn = lambda h: conv_ffn(h, f_w_up[layer], f_conv_w[layer], f_conv_b[layer], f_w_down[layer])
        x = adaln_post_norm(x, c, f_mod_w[layer], f_mod_b[layer], f_ln_g[layer], f_ln_b[layer], ffn)
    return x
```

```python
import functools
import math

import jax
import jax.numpy as jnp
from jax import lax
from jax.experimental import pallas as pl
from jax.experimental.pallas import tpu as pltpu

F32 = jnp.float32
BF16 = jnp.bfloat16
I32 = jnp.int32

LANES = 128
SUBLANES = 8
VMEM_LIMIT = 56 * 1024 * 1024

CHUNK = 64
POOL_WINDOWS = (2, 4, 8, 16)
POOL_HALO = 16
DSA_HEADS = 8
DSA_HEAD_DIM = 64
IDX_HEADS = 4
IDX_DIM = 64
TOPK_MAX = 256
GDN_HEADS = 8
SHORT_CONV = 4
FFN_CONV = 3
LN_EPS = 1e-5
RMS_EPS = 1e-6
NEG_BIG = -1e30
INT_MIN = -(2 ** 31)


def _dot(a, b):
    return jnp.dot(a, b, preferred_element_type=F32)


def _dot_nt(a, b):
    return lax.dot_general(a, b, (((1,), (1,)), ((), ())), preferred_element_type=F32)


def _dot_tn(a, b):
    return lax.dot_general(a, b, (((0,), (0,)), ((), ())), preferred_element_type=F32)


def _split(a):
    hi = a.astype(BF16)
    lo = (a - hi.astype(F32)).astype(BF16)
    return hi, lo


def _sigmoid(x):
    return 1.0 / (1.0 + jnp.exp(-x))


def _silu(x):
    return x * _sigmoid(x)


def _layer_norm(z, g, b):
    mu = jnp.mean(z, axis=-1, keepdims=True)
    zc = z - mu
    var = jnp.mean(zc * zc, axis=-1, keepdims=True)
    return zc * lax.rsqrt(var + LN_EPS) * g + b


def _params(*sem):
    return pltpu.CompilerParams(dimension_semantics=sem, vmem_limit_bytes=VMEM_LIMIT)


def _mod_kernel(c_ref, w_ref, b_ref, o_ref):
    c = c_ref[...]
    a_hi, a_lo = _split(_silu(c))
    w_hi, w_lo = _split(w_ref[...])
    o_ref[...] = _dot(a_hi, w_hi) + _dot(a_hi, w_lo) + _dot(a_lo, w_hi) + b_ref[...]


def _modulation(c, mod_w, mod_b, layer):
    b, d = c.shape
    n3 = mod_w.shape[-1]
    tn = 512
    bias = mod_b.reshape(mod_b.shape[0], 1, n3)
    out = pl.pallas_call(
        _mod_kernel,
        grid=(n3 // tn,),
        in_specs=[pl.BlockSpec((b, d), lambda j: (0, 0)),
                  pl.BlockSpec((None, d, tn), lambda j: (layer, 0, j)),
                  pl.BlockSpec((None, 1, tn), lambda j: (layer, 0, j))],
        out_specs=pl.BlockSpec((b, tn), lambda j: (0, j)),
        out_shape=jax.ShapeDtypeStruct((b, n3), F32),
        compiler_params=_params("arbitrary"),
    )(c, mod_w, bias)
    return out.reshape(b, 3, d)


def _modulate(x, mod):
    return x * (1.0 + mod[1:2, :]) + mod[0:1, :]


def _even_in_kernel(x_ref, mod_ref, wm_ref, wih_ref, wil_ref, kvn_ref, wuk_ref,
                    u_ref, ql_ref, ckv_ref, qi_ref, kw_ref, *, pool_w, dsa_w, kv_rank):
    h = _modulate(x_ref[...], mod_ref[...])
    h_hi, h_lo = _split(h)
    main = _dot(h_hi, wm_ref[...])
    u_ref[...] = main[:, :pool_w]
    q = main[:, pool_w:pool_w + dsa_w]
    ql_ref[...] = _dot(q.astype(BF16), wuk_ref[...]).astype(BF16)
    ckv = main[:, pool_w + dsa_w:]
    ms = jnp.mean(ckv * ckv, axis=-1, keepdims=True)
    ckv_ref[...] = (ckv * lax.rsqrt(ms + RMS_EPS) * kvn_ref[...]).astype(BF16)
    idx = _dot(h_hi, wih_ref[...]) + _dot(h_hi, wil_ref[...]) + _dot(h_lo, wih_ref[...])
    nqi = IDX_HEADS * IDX_DIM
    qi_ref[...] = idx[:, :nqi]
    kw_ref[...] = idx[:, nqi:]


def _even_in_proj(x, mods, w_in, kv_norm, w_uk):
    b, s, d = x.shape
    pool_w = len(POOL_WINDOWS) * LANES
    dsa_w = DSA_HEADS * DSA_HEAD_DIM
    kv_rank = w_uk.shape[1]
    n_main = pool_w + dsa_w + kv_rank
    nqi = IDX_HEADS * IDX_DIM
    n_idx = nqi + LANES
    w_main = w_in[:, :n_main].astype(BF16)
    w_idx = jnp.pad(w_in[:, n_main:], ((0, 0), (0, n_main + n_idx - w_in.shape[1])))
    wih, wil = _split(w_idx)
    wuk_t = jnp.swapaxes(w_uk, 1, 2) * (DSA_HEAD_DIM ** -0.5)
    eye = jnp.eye(DSA_HEADS, dtype=F32)
    wuk_bd = (eye[:, None, :, None] * wuk_t[:, :, None, :]).reshape(dsa_w, DSA_HEADS * kv_rank).astype(BF16)
    tm = min(512, s)
    row = lambda c: pl.BlockSpec((None, tm, c), lambda i, j: (i, j, 0))
    full = lambda a: pl.BlockSpec(a.shape, lambda i, j: (0,) * a.ndim)
    kvn = kv_norm.reshape(1, kv_rank)
    kern = functools.partial(_even_in_kernel, pool_w=pool_w, dsa_w=dsa_w, kv_rank=kv_rank)
    return pl.pallas_call(
        kern,
        grid=(b, s // tm),
        in_specs=[row(d), pl.BlockSpec((None, 3, d), lambda i, j: (i, 0, 0)),
                  full(w_main), full(wih), full(wil), full(kvn), full(wuk_bd)],
        out_specs=[row(pool_w), row(DSA_HEADS * kv_rank), row(kv_rank), row(nqi), row(LANES)],
        out_shape=[jax.ShapeDtypeStruct((b, s, pool_w), F32),
                   jax.ShapeDtypeStruct((b, s, DSA_HEADS * kv_rank), BF16),
                   jax.ShapeDtypeStruct((b, s, kv_rank), BF16),
                   jax.ShapeDtypeStruct((b, s, nqi), F32),
                   jax.ShapeDtypeStruct((b, s, LANES), F32)],
        compiler_params=_params("arbitrary", "arbitrary"),
    )(x, mods, w_main, wih, wil, kvn, wuk_bd)


def _pool_kernel(u_ref, halo_ref, w_ref, scale_ref, o_ref, ext_ref, *, ts):
    i = pl.program_id(1)
    ext_ref[0:POOL_HALO, :] = jnp.where(i > 0, halo_ref[...], 0.0)
    ext_ref[POOL_HALO:POOL_HALO + ts, :] = u_ref[...]
    t = i * ts + lax.broadcasted_iota(I32, (ts, 1), 0)
    for g, win in enumerate(POOL_WINDOWS):
        cols = slice(g * LANES, (g + 1) * LANES)
        cur = ext_ref[POOL_HALO:POOL_HALO + ts, cols]
        acc = cur
        for j in range(1, win):
            acc = acc + ext_ref[POOL_HALO - j:POOL_HALO - j + ts, cols]
        cnt = jnp.minimum(t + 1, win).astype(F32)
        pooled = acc / cnt - cur
        y = _dot(pooled.astype(BF16), w_ref[g]) * scale_ref[:, cols]
        o_ref[:, cols] = y.astype(BF16)


def _pool_mixer(u, pool_w, pool_scale):
    b, s, width = u.shape
    ts = min(512, s)
    kern = functools.partial(_pool_kernel, ts=ts)
    hb = ts // POOL_HALO
    return pl.pallas_call(
        kern,
        grid=(b, s // ts),
        in_specs=[pl.BlockSpec((None, ts, width), lambda i, j: (i, j, 0)),
                  pl.BlockSpec((None, POOL_HALO, width), lambda i, j: (i, jnp.maximum(j * hb - 1, 0), 0)),
                  pl.BlockSpec(pool_w.shape, lambda i, j: (0, 0, 0)),
                  pl.BlockSpec((1, width), lambda i, j: (0, 0))],
        out_specs=pl.BlockSpec((None, ts, width), lambda i, j: (i, j, 0)),
        out_shape=jax.ShapeDtypeStruct((b, s, width), BF16),
        scratch_shapes=[pltpu.VMEM((ts + POOL_HALO, width), F32)],
        compiler_params=_params("arbitrary", "arbitrary"),
    )(u, u, pool_w.astype(BF16), pool_scale.reshape(1, width))


def _dsa_kernel(ql_ref, qi_ref, kwq_ref, kw_ref, ckv_ref, tri_ref, wuv_ref, o_ref,
                ki3_ref, keys_ref, qs_ref, m_ref, l_ref, acc_ref, *, t, sk, topk, rank):
    qb = pl.program_id(1)
    s = kw_ref.shape[0]
    lane = lax.broadcasted_iota(I32, (1, LANES), 1)
    low = lane < IDX_DIM

    @pl.when(qb == 0)
    def _():
        k_hi, k_lo = _split(jnp.where(low, kw_ref[...], 0.0))
        hi_f = k_hi.astype(F32)
        ki3_ref[:, 0:LANES] = (hi_f + pltpu.roll(hi_f, IDX_DIM, 1)).astype(BF16)
        ki3_ref[:, LANES:2 * LANES] = k_lo

    parts = []
    for h in range(IDX_HEADS):
        xq = qi_ref[:, (h // 2) * LANES:(h // 2 + 1) * LANES]
        if h % 2:
            xq = pltpu.roll(xq, IDX_DIM, 1)
        q_hi, q_lo = _split(jnp.where(low, xq, 0.0))
        hi_f = q_hi.astype(F32)
        a = (hi_f + pltpu.roll(q_lo.astype(F32), IDX_DIM, 1)).astype(BF16)
        parts.append(jnp.concatenate([a, q_hi], axis=1))
    qi3 = jnp.concatenate(parts, axis=0)

    wscale = IDX_HEADS ** -0.5 * IDX_DIM ** -0.5
    wi = [kwq_ref[:, IDX_DIM + h:IDX_DIM + h + 1] * wscale for h in range(IDX_HEADS)]

    row = lax.broadcasted_iota(I32, (t, 1), 0)
    adm = qb * t + (row // CHUNK + 1) * CHUNK
    nkb = (qb * t + t + sk - 1) // sk
    col0 = lax.broadcasted_iota(I32, (1, sk), 1)

    def score_body(j, carry):
        off = pl.multiple_of(j * sk, sk)
        sc = _dot_nt(qi3, ki3_ref[pl.ds(off, sk), :])
        score = jnp.zeros((t, sk), F32)
        for h in range(IDX_HEADS):
            score = score + wi[h] * jnp.maximum(sc[h * t:(h + 1) * t, :], 0.0)
        score = jnp.where(score == 0.0, 0.0, score)
        bits = lax.bitcast_convert_type(score, I32)
        key = bits ^ (lax.shift_right_arithmetic(bits, jnp.int32(31)) & jnp.int32(0x7FFFFFFF))
        keys_ref[:, pl.ds(off, sk)] = jnp.where(col0 + off < adm, key, jnp.int32(INT_MIN))
        return carry

    i0 = jnp.int32(0)
    lax.fori_loop(i0, nkb, score_body, 0)

    def count(pred):
        def body(j, acc):
            off = pl.multiple_of(j * sk, sk)
            hit = pred(keys_ref[:, pl.ds(off, sk)])
            return acc + jnp.sum(hit.astype(F32), axis=1, keepdims=True)
        return lax.fori_loop(i0, nkb, body, jnp.zeros((t, 1), F32))

    kf = float(topk)
    zero = jnp.zeros((t, 1), I32)
    res0 = jnp.where(count(lambda k: k >= zero) >= kf, zero, jnp.int32(INT_MIN))

    def bit_body(i, res):
        cand = res | lax.shift_left(jnp.int32(1), jnp.int32(30) - i)
        return jnp.where(count(lambda k: k >= cand) >= kf, cand, res)

    kth = lax.fori_loop(i0, jnp.int32(31), bit_body, res0)
    need = kf - count(lambda k: k > kth)

    for h in range(DSA_HEADS):
        qs_ref[h * t:(h + 1) * t, :] = ql_ref[:, h * rank:(h + 1) * rank]
    m_ref[...] = jnp.full(m_ref.shape, NEG_BIG, F32)
    l_ref[...] = jnp.zeros(l_ref.shape, F32)
    acc_ref[...] = jnp.zeros(acc_ref.shape, F32)

    def attn_body(j, seen):
        off = pl.multiple_of(j * sk, sk)
        key = keys_ref[:, pl.ds(off, sk)]
        eq = key == kth
        rank_eq = seen + _dot(eq.astype(BF16), tri_ref[...])
        sel = ((key > kth) | (eq & (rank_eq <= need))) & (col0 + off < adm)
        kv = ckv_ref[pl.ds(off, sk), :]
        logits = _dot_nt(qs_ref[...], kv).reshape(DSA_HEADS, t, sk)
        logits = jnp.where(sel[None], logits, NEG_BIG).reshape(DSA_HEADS * t, sk)
        m_old = m_ref[...]
        m_new = jnp.maximum(m_old, jnp.max(logits, axis=1, keepdims=True))
        alpha = jnp.exp(m_old - m_new)
        p = jnp.exp(logits - m_new)
        l_ref[...] = alpha * l_ref[...] + jnp.sum(p, axis=1, keepdims=True)
        acc_ref[...] = alpha * acc_ref[...] + _dot(p.astype(BF16), kv)
        m_ref[...] = m_new
        return seen + jnp.sum(eq.astype(F32), axis=1, keepdims=True)

    lax.fori_loop(i0, nkb, attn_body, jnp.zeros((t, 1), F32))

    o = acc_ref[...] / l_ref[...]
    y = jnp.zeros(o_ref.shape, F32)
    for h in range(DSA_HEADS):
        y = y + _dot(o[h * t:(h + 1) * t, :].astype(BF16), wuv_ref[h])
    o_ref[...] = y.astype(BF16)


def _dsa_attention(ql, qi, kw, ckv, w_uv):
    b, s, _ = ql.shape
    rank = ckv.shape[-1]
    t = 2 * CHUNK
    sk = min(512, s)
    topk = min(TOPK_MAX, s // 4)
    width = DSA_HEADS * DSA_HEAD_DIM
    tri = (lax.broadcasted_iota(I32, (sk, sk), 0) <= lax.broadcasted_iota(I32, (sk, sk), 1)).astype(BF16)
    eye = jnp.eye(DSA_HEADS, dtype=F32)
    wuv = (w_uv[:, :, None, :] * eye[:, None, :, None]).reshape(DSA_HEADS, rank, width).astype(BF16)
    kern = functools.partial(_dsa_kernel, t=t, sk=sk, topk=topk, rank=rank)
    blk = lambda c: pl.BlockSpec((None, t, c), lambda i, j: (i, j, 0))
    seq = lambda c: pl.BlockSpec((None, s, c), lambda i, j: (i, 0, 0))
    return pl.pallas_call(
        kern,
        grid=(b, s // t),
        in_specs=[blk(DSA_HEADS * rank), blk(qi.shape[-1]), blk(LANES), seq(LANES), seq(rank),
                  pl.BlockSpec((sk, sk), lambda i, j: (0, 0)),
                  pl.BlockSpec(wuv.shape, lambda i, j: (0, 0, 0))],
        out_specs=blk(width),
        out_shape=jax.ShapeDtypeStruct((b, s, width), BF16),
        scratch_shapes=[pltpu.VMEM((s, 2 * LANES), BF16),
                        pltpu.VMEM((t, s), I32),
                        pltpu.VMEM((DSA_HEADS * t, rank), BF16),
                        pltpu.VMEM((DSA_HEADS * t, 1), F32),
                        pltpu.VMEM((DSA_HEADS * t, 1), F32),
                        pltpu.VMEM((DSA_HEADS * t, rank), F32)],
        compiler_params=_params("arbitrary", "arbitrary"),
    )(ql, qi, kw, kw, ckv, tri, wuv)


def _resln_kernel(*refs, n_y, widths, alpha):
    x_ref, mod_ref = refs[0], refs[1]
    y_refs = refs[2:2 + n_y]
    w_ref, g_ref, b_ref, o_ref = refs[2 + n_y:]
    acc = None
    off = 0
    for y_ref, wd in zip(y_refs, widths):
        part = _dot(y_ref[...], w_ref[off:off + wd, :])
        acc = part if acc is None else acc + part
        off += wd
    z = alpha * x_ref[...] + mod_ref[2:3, :] * acc
    o_ref[...] = _layer_norm(z, g_ref[...], b_ref[...])


def _out_proj_ln(x, mods, ys, w, ln_g, ln_b, alpha):
    b, s, d = x.shape
    widths = tuple(y.shape[-1] for y in ys)
    tm = min(512, s)
    kern = functools.partial(_resln_kernel, n_y=len(ys), widths=widths, alpha=alpha)
    row = lambda c: pl.BlockSpec((None, tm, c), lambda i, j: (i, j, 0))
    vec = pl.BlockSpec((1, d), lambda i, j: (0, 0))
    return pl.pallas_call(
        kern,
        grid=(b, s // tm),
        in_specs=[row(d), pl.BlockSpec((None, 3, d), lambda i, j: (i, 0, 0))]
                 + [row(wd) for wd in widths]
                 + [pl.BlockSpec(w.shape, lambda i, j: (0, 0)), vec, vec],
        out_specs=row(d),
        out_shape=jax.ShapeDtypeStruct((b, s, d), F32),
        compiler_params=_params("arbitrary", "arbitrary"),
    )(x, mods, *ys, w.astype(BF16), ln_g.reshape(1, d), ln_b.reshape(1, d))


def _ffn_up_kernel(x_ref, halo_ref, mod_ref, wa_ref, wv_ref, cwa_ref, cwv_ref, cba_ref, cbv_ref,
                   o_ref, h_ref, up_ref, *, tm):
    i = pl.program_id(1)
    j = pl.program_id(2)

    @pl.when(j == 0)
    def _():
        mod = mod_ref[...]
        halo = jnp.where(i > 0, _modulate(halo_ref[...], mod), 0.0)
        h_ref[0:SUBLANES, :] = halo.astype(BF16)
        h_ref[SUBLANES:SUBLANES + tm, :] = _modulate(x_ref[...], mod).astype(BF16)

    def conv(w_ref, cw_ref, cb_ref):
        up_ref[...] = _dot(h_ref[...], w_ref[...])
        out = cb_ref[...]
        for k in range(FFN_CONV):
            lo = SUBLANES - (FFN_CONV - 1) + k
            out = out + cw_ref[k:k + 1, :] * up_ref[lo:lo + tm, :]
        return out

    a = conv(wa_ref, cwa_ref, cba_ref)
    v = conv(wv_ref, cwv_ref, cbv_ref)
    o_ref[...] = (_silu(a) * v).astype(BF16)


def _ffn_up(x, mods, w_up, conv_w, conv_b):
    b, s, d = x.shape
    d_ff = w_up.shape[1] // 2
    tm = min(512, s)
    tn = d_ff // 3 if d_ff % (3 * LANES) == 0 else d_ff
    ncol = d_ff // tn
    wb = w_up.astype(BF16)
    cb = conv_b.reshape(1, 2 * d_ff)
    hb = tm // SUBLANES
    kern = functools.partial(_ffn_up_kernel, tm=tm)
    col_a = lambda r: pl.BlockSpec((r, tn), lambda i, j, k: (0, k))
    col_v = lambda r: pl.BlockSpec((r, tn), lambda i, j, k: (0, k + ncol))
    return pl.pallas_call(
        kern,
        grid=(b, s // tm, ncol),
        in_specs=[pl.BlockSpec((None, tm, d), lambda i, j, k: (i, j, 0)),
                  pl.BlockSpec((None, SUBLANES, d), lambda i, j, k: (i, jnp.maximum(j * hb - 1, 0), 0)),
                  pl.BlockSpec((None, 3, d), lambda i, j, k: (i, 0, 0)),
                  col_a(d), col_v(d), col_a(FFN_CONV), col_v(FFN_CONV), col_a(1), col_v(1)],
        out_specs=pl.BlockSpec((None, tm, tn), lambda i, j, k: (i, j, k)),
        out_shape=jax.ShapeDtypeStruct((b, s, d_ff), BF16),
        scratch_shapes=[pltpu.VMEM((tm + SUBLANES, d), BF16),
                        pltpu.VMEM((tm + SUBLANES, tn), F32)],
        compiler_params=_params("arbitrary", "arbitrary", "arbitrary"),
    )(x, x, mods, wb, wb, conv_w, conv_w, cb, cb)


def _odd_in_kernel(x_ref, mod_ref, wq_ref, wg_ref, wbh_ref, wbl_ref, qkv_ref, gate_ref, ba_ref):
    h = _modulate(x_ref[...], mod_ref[...])
    h_hi, h_lo = _split(h)
    qkv_ref[...] = _dot(h_hi, wq_ref[...])
    gate_ref[...] = _dot(h_hi, wg_ref[...])
    ba_ref[...] = _dot(h_hi, wbh_ref[...]) + _dot(h_hi, wbl_ref[...]) + _dot(h_lo, wbh_ref[...])


def _odd_in_proj(x, mods, w_in, width):
    b, s, d = x.shape
    wq = w_in[:, :3 * width].astype(BF16)
    wg = w_in[:, 3 * width:4 * width].astype(BF16)
    wba = jnp.pad(w_in[:, 4 * width:], ((0, 0), (0, LANES - 2 * GDN_HEADS)))
    wbh, wbl = _split(wba)
    tm = min(256, s)
    row = lambda c: pl.BlockSpec((None, tm, c), lambda i, j: (i, j, 0))
    full = lambda a: pl.BlockSpec(a.shape, lambda i, j: (0, 0))
    return pl.pallas_call(
        _odd_in_kernel,
        grid=(b, s // tm),
        in_specs=[row(d), pl.BlockSpec((None, 3, d), lambda i, j: (i, 0, 0)),
                  full(wq), full(wg), full(wbh), full(wbl)],
        out_specs=[row(3 * width), row(width), row(LANES)],
        out_shape=[jax.ShapeDtypeStruct((b, s, 3 * width), F32),
                   jax.ShapeDtypeStruct((b, s, width), F32),
                   jax.ShapeDtypeStruct((b, s, LANES), F32)],
        compiler_params=_params("arbitrary", "arbitrary"),
    )(x, mods, wq, wg, wbh, wbl)


def _unit_lower_inverse_minus_eye(a, ii, jj):
    mm = lambda p, q: _dot(p.astype(BF16), q.astype(BF16))
    diag = (ii // 16) == (jj // 16)
    ad = jnp.where(diag, a, 0.0)
    e = a - ad
    a2 = mm(ad, ad)
    a4 = mm(a2, a2)
    a8 = mm(a4, a4)
    p = -ad
    for sq in (a2, a4, a8):
        p = p + sq + mm(p, sq)
    m = e + mm(p, e)
    m2 = mm(m, m)
    q = m2 - m - mm(m, m2)
    return q + p + mm(q, p)


def _gdn_kernel(qkv_ref, gate_ref, ba_ref, cw_ref, alog_ref, dtb_ref, onorm_ref, ltri_ref, y_ref,
                ext_ref, act_ref, state_ref, *, ts, width, dh):
    i = pl.program_id(1)
    nh = width // dh

    @pl.when(i == 0)
    def _():
        ext_ref[0:SUBLANES, :] = jnp.zeros((SUBLANES, 3 * width), F32)
        state_ref[...] = jnp.zeros(state_ref.shape, F32)

    @pl.when(i > 0)
    def _():
        ext_ref[0:SUBLANES, :] = ext_ref[ts:ts + SUBLANES, :]

    ext_ref[SUBLANES:SUBLANES + ts, :] = qkv_ref[...]

    for cb in range(3 * nh):
        cols = slice(cb * dh, (cb + 1) * dh)
        acc = None
        for k in range(SHORT_CONV):
            lo = SUBLANES - (SHORT_CONV - 1) + k
            term = cw_ref[k:k + 1, cols] * ext_ref[lo:lo + ts, cols]
            acc = term if acc is None else acc + term
        z = _silu(acc)
        if cb < 2 * nh:
            z = z * lax.rsqrt(jnp.sum(z * z, axis=-1, keepdims=True) + RMS_EPS)
            if cb < nh:
                z = z * (dh ** -0.5)
        act_ref[:, cols] = z

    ii = lax.broadcasted_iota(I32, (CHUNK, CHUNK), 0)
    jj = lax.broadcasted_iota(I32, (CHUNK, CHUNK), 1)
    ltri = ltri_ref[...]

    def chunk_body(c, carry):
        r0 = pl.multiple_of(c * CHUNK, CHUNK)
        rows = pl.ds(r0, CHUNK)
        ba = ba_ref[rows, :]
        beta = _sigmoid(ba)
        xg = ba + dtb_ref[...]
        softplus = jnp.maximum(xg, 0.0) + jnp.log1p(jnp.exp(-jnp.abs(xg)))
        g = -jnp.exp(alog_ref[...]) * softplus
        g1 = g.astype(BF16)
        r1 = g - g1.astype(F32)
        g2 = r1.astype(BF16)
        g3 = (r1 - g2.astype(F32)).astype(BF16)
        gc = _dot(ltri, g1) + _dot(ltri, g2) + _dot(ltri, g3)
        gc_t = gc.T
        egc = jnp.exp(gc)
        g_last = gc[CHUNK - 1:CHUNK, :]
        e_last = jnp.exp(g_last)
        k_fac = jnp.exp(g_last - gc)

        for h in range(nh):
            hc = slice(h * dh, (h + 1) * dh)
            q = act_ref[rows, h * dh:(h + 1) * dh]
            k = act_ref[rows, width + h * dh:width + (h + 1) * dh]
            v = act_ref[rows, 2 * width + h * dh:2 * width + (h + 1) * dh]
            b_h = beta[:, h:h + 1]
            gl = nh + h
            diff = gc[:, gl:gl + 1] - gc_t[gl:gl + 1, :]
            dec = jnp.exp(jnp.where(ii >= jj, diff, 0.0))
            kb = k * b_h
            k_bf = k.astype(BF16)
            a = _dot_nt(kb.astype(BF16), k_bf) * jnp.where(ii > jj, dec, 0.0)
            attn = _dot_nt(q.astype(BF16), k_bf) * jnp.where(ii >= jj, dec, 0.0)
            tm1 = _unit_lower_inverse_minus_eye(a, ii, jj).astype(BF16)
            vb = v * b_h
            kbg = kb * egc[:, gl:gl + 1]
            u = vb + _dot(tm1, vb.astype(BF16))
            w = kbg + _dot(tm1, kbg.astype(BF16))
            st = state_ref[h]
            st_bf = st.astype(BF16)
            v_new = u - _dot(w.astype(BF16), st_bf)
            vn_bf = v_new.astype(BF16)
            o = _dot((q * egc[:, gl:gl + 1]).astype(BF16), st_bf) + _dot(attn.astype(BF16), vn_bf)
            kd = (k * k_fac[:, gl:gl + 1]).astype(BF16)
            state_ref[h] = st * e_last[:, gl:gl + 1] + _dot_tn(kd, vn_bf)
            on = o * lax.rsqrt(jnp.mean(o * o, axis=-1, keepdims=True) + RMS_EPS) * onorm_ref[...]
            y_ref[rows, hc] = (on * _silu(gate_ref[rows, hc])).astype(BF16)
        return carry

    lax.fori_loop(jnp.int32(0), jnp.int32(ts // CHUNK), chunk_body, 0)


def _gdn(qkv, gate, ba, conv_w, a_log, dt_bias, out_norm):
    b, s, w3 = qkv.shape
    width = w3 // 3
    dh = width // GDN_HEADS
    ts = min(512, s)
    pad_a = lambda v: jnp.pad(v.reshape(1, GDN_HEADS), ((0, 0), (GDN_HEADS, LANES - 2 * GDN_HEADS)))
    ltri = (lax.broadcasted_iota(I32, (CHUNK, CHUNK), 0) >= lax.broadcasted_iota(I32, (CHUNK, CHUNK), 1)).astype(BF16)
    kern = functools.partial(_gdn_kernel, ts=ts, width=width, dh=dh)
    row = lambda c: pl.BlockSpec((None, ts, c), lambda i, j: (i, j, 0))
    full = lambda shape: pl.BlockSpec(shape, lambda i, j: (0,) * len(shape))
    return pl.pallas_call(
        kern,
        grid=(b, s // ts),
        in_specs=[row(w3), row(width), row(LANES), full(conv_w.shape), full((1, LANES)), full((1, LANES)),
                  full((1, dh)), full((CHUNK, CHUNK))],
        out_specs=row(width),
        out_shape=jax.ShapeDtypeStruct((b, s, width), BF16),
        scratch_shapes=[pltpu.VMEM((ts + SUBLANES, w3), F32),
                        pltpu.VMEM((ts, w3), F32),
                        pltpu.VMEM((GDN_HEADS, dh, dh), F32)],
        compiler_params=_params("arbitrary", "arbitrary"),
    )(qkv, gate, ba, conv_w, pad_a(a_log), pad_a(dt_bias), out_norm.reshape(1, dh), ltri)


def kernel(x, c, e_mod_w, e_mod_b, e_w_in, e_pool_w, e_pool_scale, e_kv_norm, e_w_uk, e_w_uv, e_w_out, e_ln_g, e_ln_b, o_mod_w, o_mod_b, o_w_in, o_conv_w, o_a_log, o_dt_bias, o_out_norm, o_w_out, o_ln_g, o_ln_b, f_mod_w, f_mod_b, f_w_up, f_conv_w, f_conv_b, f_w_down, f_ln_g, f_ln_b):
    depth = f_mod_w.shape[0]
    alpha = (2 * depth) ** 0.25
    for layer in range(depth):
        i = layer // 2
        if layer % 2 == 0:
            mods = _modulation(c, e_mod_w, e_mod_b, i)
            u, ql, ckv, qi, kw = _even_in_proj(x, mods, e_w_in[i], e_kv_norm[i], e_w_uk[i])
            y_pool = _pool_mixer(u, e_pool_w[i], e_pool_scale[i])
            y_dsa = _dsa_attention(ql, qi, kw, ckv, e_w_uv[i])
            x = _out_proj_ln(x, mods, [y_pool, y_dsa], e_w_out[i], e_ln_g[i], e_ln_b[i], alpha)
        else:
            mods = _modulation(c, o_mod_w, o_mod_b, i)
            width = o_w_out.shape[1]
            qkv, gate, ba = _odd_in_proj(x, mods, o_w_in[i], width)
            y = _gdn(qkv, gate, ba, o_conv_w[i], o_a_log[i], o_dt_bias[i], o_out_norm[i])
            x = _out_proj_ln(x, mods, [y], o_w_out[i], o_ln_g[i], o_ln_b[i], alpha)
        mods = _modulation(c, f_mod_w, f_mod_b, layer)
        act = _ffn_up(x, mods, f_w_up[layer], f_conv_w[layer], f_conv_b[layer])
        x = _out_proj_ln(x, mods, [act], f_w_down[layer], f_ln_g[layer], f_ln_b[layer], alpha)
    return x
```

```python
import functools
import math

import jax
import jax.numpy as jnp
from jax import lax
from jax.experimental import pallas as pl
from jax.experimental.pallas import tpu as pltpu

F32 = jnp.float32
BF16 = jnp.bfloat16
I32 = jnp.int32

LANES = 128
SUBLANES = 8
VMEM_LIMIT = 56 * 1024 * 1024

CHUNK = 64
POOL_WINDOWS = (2, 4, 8, 16)
POOL_HALO = 16
DSA_HEADS = 8
DSA_HEAD_DIM = 64
IDX_HEADS = 4
IDX_DIM = 64
TOPK_MAX = 256
GDN_HEADS = 8
GDN_GROUP = 2
SHORT_CONV = 4
FFN_CONV = 3
LN_EPS = 1e-5
RMS_EPS = 1e-6
NEG_BIG = -1e30
INT_MIN = -(2 ** 31)


def _dot(a, b):
    return jnp.dot(a, b, preferred_element_type=F32)


def _dot_nt(a, b):
    return lax.dot_general(a, b, (((1,), (1,)), ((), ())), preferred_element_type=F32)


def _dot_tn(a, b):
    return lax.dot_general(a, b, (((0,), (0,)), ((), ())), preferred_element_type=F32)


def _split(a):
    hi = a.astype(BF16)
    lo = (a - hi.astype(F32)).astype(BF16)
    return hi, lo


def _sigmoid(x):
    return 1.0 / (1.0 + jnp.exp(-x))


def _silu(x):
    return x * _sigmoid(x)


def _layer_norm(z, g, b):
    mu = jnp.mean(z, axis=-1, keepdims=True)
    zc = z - mu
    var = jnp.mean(zc * zc, axis=-1, keepdims=True)
    return zc * lax.rsqrt(var + LN_EPS) * g + b


def _params(*sem):
    return pltpu.CompilerParams(dimension_semantics=sem, vmem_limit_bytes=VMEM_LIMIT)


def _mod_kernel(c_ref, w_ref, b_ref, o_ref):
    c = c_ref[...]
    a_hi, a_lo = _split(_silu(c))
    w_hi, w_lo = _split(w_ref[...])
    o_ref[...] = _dot(a_hi, w_hi) + _dot(a_hi, w_lo) + _dot(a_lo, w_hi) + b_ref[...]


def _modulation(c, mod_w, mod_b, layer):
    b, d = c.shape
    n3 = mod_w.shape[-1]
    tn = 512
    bias = mod_b.reshape(mod_b.shape[0], 1, n3)
    out = pl.pallas_call(
        _mod_kernel,
        grid=(n3 // tn,),
        in_specs=[pl.BlockSpec((b, d), lambda j: (0, 0)),
                  pl.BlockSpec((None, d, tn), lambda j: (layer, 0, j)),
                  pl.BlockSpec((None, 1, tn), lambda j: (layer, 0, j))],
        out_specs=pl.BlockSpec((b, tn), lambda j: (0, j)),
        out_shape=jax.ShapeDtypeStruct((b, n3), F32),
        compiler_params=_params("arbitrary"),
    )(c, mod_w, bias)
    return out.reshape(b, 3, d)


def _modulate(x, mod):
    return x * (1.0 + mod[1:2, :]) + mod[0:1, :]


def _even_in_kernel(x_ref, mod_ref, wm_ref, wih_ref, wil_ref, kvn_ref, wuk_ref,
                    u_ref, ql_ref, ckv_ref, qi_ref, kw_ref, *, pool_w, dsa_w, kv_rank):
    h = _modulate(x_ref[...], mod_ref[...])
    h_hi, h_lo = _split(h)
    main = _dot(h_hi, wm_ref[...])
    u_ref[...] = main[:, :pool_w]
    q = main[:, pool_w:pool_w + dsa_w]
    ql_ref[...] = _dot(q.astype(BF16), wuk_ref[...]).astype(BF16)
    ckv = main[:, pool_w + dsa_w:]
    ms = jnp.mean(ckv * ckv, axis=-1, keepdims=True)
    ckv_ref[...] = (ckv * lax.rsqrt(ms + RMS_EPS) * kvn_ref[...]).astype(BF16)
    idx = _dot(h_hi, wih_ref[...]) + _dot(h_hi, wil_ref[...]) + _dot(h_lo, wih_ref[...])
    nqi = IDX_HEADS * IDX_DIM
    qi_ref[...] = idx[:, :nqi]
    kw_ref[...] = idx[:, nqi:]


def _even_in_proj(x, mods, w_in, kv_norm, w_uk):
    b, s, d = x.shape
    pool_w = len(POOL_WINDOWS) * LANES
    dsa_w = DSA_HEADS * DSA_HEAD_DIM
    kv_rank = w_uk.shape[1]
    n_main = pool_w + dsa_w + kv_rank
    nqi = IDX_HEADS * IDX_DIM
    n_idx = nqi + LANES
    w_main = w_in[:, :n_main].astype(BF16)
    w_idx = jnp.pad(w_in[:, n_main:], ((0, 0), (0, n_main + n_idx - w_in.shape[1])))
    wih, wil = _split(w_idx)
    wuk_t = jnp.swapaxes(w_uk, 1, 2) * (DSA_HEAD_DIM ** -0.5)
    eye = jnp.eye(DSA_HEADS, dtype=F32)
    wuk_bd = (eye[:, None, :, None] * wuk_t[:, :, None, :]).reshape(dsa_w, DSA_HEADS * kv_rank).astype(BF16)
    tm = min(512, s)
    row = lambda c: pl.BlockSpec((None, tm, c), lambda i, j: (i, j, 0))
    full = lambda a: pl.BlockSpec(a.shape, lambda i, j: (0,) * a.ndim)
    kvn = kv_norm.reshape(1, kv_rank)
    kern = functools.partial(_even_in_kernel, pool_w=pool_w, dsa_w=dsa_w, kv_rank=kv_rank)
    return pl.pallas_call(
        kern,
        grid=(b, s // tm),
        in_specs=[row(d), pl.BlockSpec((None, 3, d), lambda i, j: (i, 0, 0)),
                  full(w_main), full(wih), full(wil), full(kvn), full(wuk_bd)],
        out_specs=[row(pool_w), row(DSA_HEADS * kv_rank), row(kv_rank), row(nqi), row(LANES)],
        out_shape=[jax.ShapeDtypeStruct((b, s, pool_w), F32),
                   jax.ShapeDtypeStruct((b, s, DSA_HEADS * kv_rank), BF16),
                   jax.ShapeDtypeStruct((b, s, kv_rank), BF16),
                   jax.ShapeDtypeStruct((b, s, nqi), F32),
                   jax.ShapeDtypeStruct((b, s, LANES), F32)],
        compiler_params=_params("arbitrary", "arbitrary"),
    )(x, mods, w_main, wih, wil, kvn, wuk_bd)


def _pool_kernel(u_ref, halo_ref, w_ref, scale_ref, o_ref, ext_ref, *, ts):
    i = pl.program_id(1)
    ext_ref[0:POOL_HALO, :] = jnp.where(i > 0, halo_ref[...], 0.0)
    ext_ref[POOL_HALO:POOL_HALO + ts, :] = u_ref[...]
    t = i * ts + lax.broadcasted_iota(I32, (ts, 1), 0)
    for g, win in enumerate(POOL_WINDOWS):
        cols = slice(g * LANES, (g + 1) * LANES)
        cur = ext_ref[POOL_HALO:POOL_HALO + ts, cols]
        acc = cur
        for j in range(1, win):
            acc = acc + ext_ref[POOL_HALO - j:POOL_HALO - j + ts, cols]
        cnt = jnp.minimum(t + 1, win).astype(F32)
        pooled = acc / cnt - cur
        y = _dot(pooled.astype(BF16), w_ref[g]) * scale_ref[:, cols]
        o_ref[:, cols] = y.astype(BF16)


def _pool_mixer(u, pool_w, pool_scale):
    b, s, width = u.shape
    ts = min(512, s)
    kern = functools.partial(_pool_kernel, ts=ts)
    hb = ts // POOL_HALO
    return pl.pallas_call(
        kern,
        grid=(b, s // ts),
        in_specs=[pl.BlockSpec((None, ts, width), lambda i, j: (i, j, 0)),
                  pl.BlockSpec((None, POOL_HALO, width), lambda i, j: (i, jnp.maximum(j * hb - 1, 0), 0)),
                  pl.BlockSpec(pool_w.shape, lambda i, j: (0, 0, 0)),
                  pl.BlockSpec((1, width), lambda i, j: (0, 0))],
        out_specs=pl.BlockSpec((None, ts, width), lambda i, j: (i, j, 0)),
        out_shape=jax.ShapeDtypeStruct((b, s, width), BF16),
        scratch_shapes=[pltpu.VMEM((ts + POOL_HALO, width), F32)],
        compiler_params=_params("arbitrary", "arbitrary"),
    )(u, u, pool_w.astype(BF16), pool_scale.reshape(1, width))


def _dsa_kernel(ql_ref, qi_ref, kwq_ref, kw_ref, ckv_ref, tri_ref, wuv_ref, o_ref,
                ki3_ref, keys_ref, qs_ref, m_ref, l_ref, acc_ref, *, t, sk, topk, rank):
    qb = pl.program_id(1)
    s = kw_ref.shape[0]
    lane = lax.broadcasted_iota(I32, (1, LANES), 1)
    low = lane < IDX_DIM

    @pl.when(qb == 0)
    def _():
        k_hi, k_lo = _split(jnp.where(low, kw_ref[...], 0.0))
        hi_f = k_hi.astype(F32)
        ki3_ref[:, 0:LANES] = (hi_f + pltpu.roll(hi_f, IDX_DIM, 1)).astype(BF16)
        ki3_ref[:, LANES:2 * LANES] = k_lo

    parts = []
    for h in range(IDX_HEADS):
        xq = qi_ref[:, (h // 2) * LANES:(h // 2 + 1) * LANES]
        if h % 2:
            xq = pltpu.roll(xq, IDX_DIM, 1)
        q_hi, q_lo = _split(jnp.where(low, xq, 0.0))
        hi_f = q_hi.astype(F32)
        a = (hi_f + pltpu.roll(q_lo.astype(F32), IDX_DIM, 1)).astype(BF16)
        parts.append(jnp.concatenate([a, q_hi], axis=1))
    qi3 = jnp.concatenate(parts, axis=0)

    wscale = IDX_HEADS ** -0.5 * IDX_DIM ** -0.5
    wi = [kwq_ref[:, IDX_DIM + h:IDX_DIM + h + 1] * wscale for h in range(IDX_HEADS)]

    row = lax.broadcasted_iota(I32, (t, 1), 0)
    adm = qb * t + (row // CHUNK + 1) * CHUNK
    nkb = (qb * t + t + sk - 1) // sk
    col0 = lax.broadcasted_iota(I32, (1, sk), 1)

    def score_body(j, carry):
        off = pl.multiple_of(j * sk, sk)
        sc = _dot_nt(qi3, ki3_ref[pl.ds(off, sk), :])
        score = jnp.zeros((t, sk), F32)
        for h in range(IDX_HEADS):
            score = score + wi[h] * jnp.maximum(sc[h * t:(h + 1) * t, :], 0.0)
        score = jnp.where(score == 0.0, 0.0, score)
        bits = lax.bitcast_convert_type(score, I32)
        key = bits ^ (lax.shift_right_arithmetic(bits, jnp.int32(31)) & jnp.int32(0x7FFFFFFF))
        keys_ref[:, pl.ds(off, sk)] = jnp.where(col0 + off < adm, key, jnp.int32(INT_MIN))
        return carry

    i0 = jnp.int32(0)
    lax.fori_loop(i0, nkb, score_body, 0)

    def count(pred, ref):
        ref_b = jnp.broadcast_to(ref, (t, LANES))

        def body(j, acc):
            off = pl.multiple_of(j * sk, sk)
            for c in range(sk // LANES):
                hit = pred(keys_ref[:, pl.ds(pl.multiple_of(off + c * LANES, LANES), LANES)], ref_b)
                acc = acc + jnp.where(hit, 1.0, 0.0)
            return acc
        acc = lax.fori_loop(i0, nkb, body, jnp.zeros((t, LANES), F32))
        return jnp.sum(acc, axis=1, keepdims=True)

    ge = lambda k, r: k >= r
    kf = float(topk)
    zero = jnp.zeros((t, 1), I32)
    res0 = jnp.where(count(ge, zero) >= kf, zero, jnp.int32(INT_MIN))

    def bit_body(i, res):
        cand = res | lax.shift_left(jnp.int32(1), jnp.int32(30) - i)
        return jnp.where(count(ge, cand) >= kf, cand, res)

    kth = lax.fori_loop(i0, jnp.int32(31), bit_body, res0)
    need = kf - count(lambda k, r: k > r, kth)

    for h in range(DSA_HEADS):
        qs_ref[h * t:(h + 1) * t, :] = ql_ref[:, h * rank:(h + 1) * rank]
    m_ref[...] = jnp.full(m_ref.shape, NEG_BIG, F32)
    l_ref[...] = jnp.zeros(l_ref.shape, F32)
    acc_ref[...] = jnp.zeros(acc_ref.shape, F32)

    def attn_body(j, seen):
        off = pl.multiple_of(j * sk, sk)
        key = keys_ref[:, pl.ds(off, sk)]
        eq = key == kth
        rank_eq = seen + _dot(eq.astype(BF16), tri_ref[...])
        sel = ((key > kth) | (eq & (rank_eq <= need))) & (col0 + off < adm)
        kv = ckv_ref[pl.ds(off, sk), :]
        logits = _dot_nt(qs_ref[...], kv).reshape(DSA_HEADS, t, sk)
        logits = jnp.where(sel[None], logits, NEG_BIG).reshape(DSA_HEADS * t, sk)
        m_old = m_ref[...]
        m_new = jnp.maximum(m_old, jnp.max(logits, axis=1, keepdims=True))
        alpha = jnp.exp(m_old - m_new)
        p = jnp.exp(logits - m_new)
        l_ref[...] = alpha * l_ref[...] + jnp.sum(p, axis=1, keepdims=True)
        acc_ref[...] = alpha * acc_ref[...] + _dot(p.astype(BF16), kv)
        m_ref[...] = m_new
        return seen + jnp.sum(eq.astype(F32), axis=1, keepdims=True)

    lax.fori_loop(i0, nkb, attn_body, jnp.zeros((t, 1), F32))

    o = acc_ref[...] / l_ref[...]
    y = jnp.zeros(o_ref.shape, F32)
    for h in range(DSA_HEADS):
        y = y + _dot(o[h * t:(h + 1) * t, :].astype(BF16), wuv_ref[h])
    o_ref[...] = y.astype(BF16)


def _dsa_attention(ql, qi, kw, ckv, w_uv):
    b, s, _ = ql.shape
    rank = ckv.shape[-1]
    t = 2 * CHUNK
    sk = min(512, s)
    topk = min(TOPK_MAX, s // 4)
    width = DSA_HEADS * DSA_HEAD_DIM
    tri = (lax.broadcasted_iota(I32, (sk, sk), 0) <= lax.broadcasted_iota(I32, (sk, sk), 1)).astype(BF16)
    eye = jnp.eye(DSA_HEADS, dtype=F32)
    wuv = (w_uv[:, :, None, :] * eye[:, None, :, None]).reshape(DSA_HEADS, rank, width).astype(BF16)
    kern = functools.partial(_dsa_kernel, t=t, sk=sk, topk=topk, rank=rank)
    blk = lambda c: pl.BlockSpec((None, t, c), lambda i, j: (i, j, 0))
    seq = lambda c: pl.BlockSpec((None, s, c), lambda i, j: (i, 0, 0))
    return pl.pallas_call(
        kern,
        grid=(b, s // t),
        in_specs=[blk(DSA_HEADS * rank), blk(qi.shape[-1]), blk(LANES), seq(LANES), seq(rank),
                  pl.BlockSpec((sk, sk), lambda i, j: (0, 0)),
                  pl.BlockSpec(wuv.shape, lambda i, j: (0, 0, 0))],
        out_specs=blk(width),
        out_shape=jax.ShapeDtypeStruct((b, s, width), BF16),
        scratch_shapes=[pltpu.VMEM((s, 2 * LANES), BF16),
                        pltpu.VMEM((t, s), I32),
                        pltpu.VMEM((DSA_HEADS * t, rank), BF16),
                        pltpu.VMEM((DSA_HEADS * t, 1), F32),
                        pltpu.VMEM((DSA_HEADS * t, 1), F32),
                        pltpu.VMEM((DSA_HEADS * t, rank), F32)],
        compiler_params=_params("arbitrary", "arbitrary"),
    )(ql, qi, kw, kw, ckv, tri, wuv)


def _resln_kernel(*refs, n_y, widths, alpha):
    x_ref, mod_ref = refs[0], refs[1]
    y_refs = refs[2:2 + n_y]
    w_ref, g_ref, b_ref, o_ref = refs[2 + n_y:]
    acc = None
    off = 0
    for y_ref, wd in zip(y_refs, widths):
        part = _dot(y_ref[...], w_ref[off:off + wd, :])
        acc = part if acc is None else acc + part
        off += wd
    z = alpha * x_ref[...] + mod_ref[2:3, :] * acc
    o_ref[...] = _layer_norm(z, g_ref[...], b_ref[...])


def _out_proj_ln(x, mods, ys, w, ln_g, ln_b, alpha):
    b, s, d = x.shape
    widths = tuple(y.shape[-1] for y in ys)
    tm = min(512, s)
    kern = functools.partial(_resln_kernel, n_y=len(ys), widths=widths, alpha=alpha)
    row = lambda c: pl.BlockSpec((None, tm, c), lambda i, j: (i, j, 0))
    vec = pl.BlockSpec((1, d), lambda i, j: (0, 0))
    return pl.pallas_call(
        kern,
        grid=(b, s // tm),
        in_specs=[row(d), pl.BlockSpec((None, 3, d), lambda i, j: (i, 0, 0))]
                 + [row(wd) for wd in widths]
                 + [pl.BlockSpec(w.shape, lambda i, j: (0, 0)), vec, vec],
        out_specs=row(d),
        out_shape=jax.ShapeDtypeStruct((b, s, d), F32),
        compiler_params=_params("arbitrary", "arbitrary"),
    )(x, mods, *ys, w.astype(BF16), ln_g.reshape(1, d), ln_b.reshape(1, d))


def _ffn_up_kernel(x_ref, halo_ref, mod_ref, wa_ref, wv_ref, cwa_ref, cwv_ref, cba_ref, cbv_ref,
                   o_ref, h_ref, up_ref, *, tm):
    i = pl.program_id(1)
    j = pl.program_id(2)

    @pl.when(j == 0)
    def _():
        mod = mod_ref[...]
        halo = jnp.where(i > 0, _modulate(halo_ref[...], mod), 0.0)
        h_ref[0:SUBLANES, :] = halo.astype(BF16)
        h_ref[SUBLANES:SUBLANES + tm, :] = _modulate(x_ref[...], mod).astype(BF16)

    def conv(w_ref, cw_ref, cb_ref):
        up_ref[...] = _dot(h_ref[...], w_ref[...])
        out = cb_ref[...]
        for k in range(FFN_CONV):
            lo = SUBLANES - (FFN_CONV - 1) + k
            out = out + cw_ref[k:k + 1, :] * up_ref[lo:lo + tm, :]
        return out

    a = conv(wa_ref, cwa_ref, cba_ref)
    v = conv(wv_ref, cwv_ref, cbv_ref)
    o_ref[...] = (_silu(a) * v).astype(BF16)


def _ffn_up(x, mods, w_up, conv_w, conv_b):
    b, s, d = x.shape
    d_ff = w_up.shape[1] // 2
    tm = min(512, s)
    tn = d_ff // 3 if d_ff % (3 * LANES) == 0 else d_ff
    ncol = d_ff // tn
    wb = w_up.astype(BF16)
    cb = conv_b.reshape(1, 2 * d_ff)
    hb = tm // SUBLANES
    kern = functools.partial(_ffn_up_kernel, tm=tm)
    col_a = lambda r: pl.BlockSpec((r, tn), lambda i, j, k: (0, k))
    col_v = lambda r: pl.BlockSpec((r, tn), lambda i, j, k: (0, k + ncol))
    return pl.pallas_call(
        kern,
        grid=(b, s // tm, ncol),
        in_specs=[pl.BlockSpec((None, tm, d), lambda i, j, k: (i, j, 0)),
                  pl.BlockSpec((None, SUBLANES, d), lambda i, j, k: (i, jnp.maximum(j * hb - 1, 0), 0)),
                  pl.BlockSpec((None, 3, d), lambda i, j, k: (i, 0, 0)),
                  col_a(d), col_v(d), col_a(FFN_CONV), col_v(FFN_CONV), col_a(1), col_v(1)],
        out_specs=pl.BlockSpec((None, tm, tn), lambda i, j, k: (i, j, k)),
        out_shape=jax.ShapeDtypeStruct((b, s, d_ff), BF16),
        scratch_shapes=[pltpu.VMEM((tm + SUBLANES, d), BF16),
                        pltpu.VMEM((tm + SUBLANES, tn), F32)],
        compiler_params=_params("arbitrary", "arbitrary", "arbitrary"),
    )(x, x, mods, wb, wb, conv_w, conv_w, cb, cb)


def _odd_in_kernel(x_ref, mod_ref, wq_ref, wg_ref, wbh_ref, wbl_ref, qkv_ref, gate_ref, ba_ref):
    h = _modulate(x_ref[...], mod_ref[...])
    h_hi, h_lo = _split(h)
    qkv_ref[...] = _dot(h_hi, wq_ref[...])
    gate_ref[...] = _dot(h_hi, wg_ref[...])
    ba_ref[...] = _dot(h_hi, wbh_ref[...]) + _dot(h_hi, wbl_ref[...]) + _dot(h_lo, wbh_ref[...])


def _odd_in_proj(x, mods, w_in, width):
    b, s, d = x.shape
    wq = w_in[:, :3 * width].astype(BF16)
    wg = w_in[:, 3 * width:4 * width].astype(BF16)
    wba = jnp.pad(w_in[:, 4 * width:], ((0, 0), (0, LANES - 2 * GDN_HEADS)))
    wbh, wbl = _split(wba)
    tm = min(256, s)
    row = lambda c: pl.BlockSpec((None, tm, c), lambda i, j: (i, j, 0))
    full = lambda a: pl.BlockSpec(a.shape, lambda i, j: (0, 0))
    return pl.pallas_call(
        _odd_in_kernel,
        grid=(b, s // tm),
        in_specs=[row(d), pl.BlockSpec((None, 3, d), lambda i, j: (i, 0, 0)),
                  full(wq), full(wg), full(wbh), full(wbl)],
        out_specs=[row(3 * width), row(width), row(LANES)],
        out_shape=[jax.ShapeDtypeStruct((b, s, 3 * width), F32),
                   jax.ShapeDtypeStruct((b, s, width), F32),
                   jax.ShapeDtypeStruct((b, s, LANES), F32)],
        compiler_params=_params("arbitrary", "arbitrary"),
    )(x, mods, wq, wg, wbh, wbl)


def _bmm(a, b):
    return lax.dot_general(a, b, (((2,), (1,)), ((0,), (0,))), preferred_element_type=F32)


def _bmm_nt(a, b):
    return lax.dot_general(a, b, (((2,), (2,)), ((0,), (0,))), preferred_element_type=F32)


def _unit_lower_inverse_minus_eye(a, ii, jj):
    mm = lambda p, q: _bmm(p.astype(BF16), q.astype(BF16))
    diag = (ii // 16) == (jj // 16)
    ad = jnp.where(diag, a, 0.0)
    e = a - ad
    a2 = mm(ad, ad)
    a4 = mm(a2, a2)
    a8 = mm(a4, a4)
    p = -ad
    for sq in (a2, a4, a8):
        p = p + sq + mm(p, sq)
    m = e + mm(p, e)
    m2 = mm(m, m)
    q = m2 - m - mm(m, m2)
    return q + p + mm(q, p)


def _gdn_kernel(qkv_ref, gate_ref, ba_ref, cw_ref, alog_ref, dtb_ref, onorm_ref, ltri_ref, y_ref,
                ext_ref, act_ref, state_ref, u_ref, w_ref, attn_ref, qg_ref, kdt_ref, el_ref, *, ts, width, dh):
    i = pl.program_id(1)
    nh = width // dh

    @pl.when(i == 0)
    def _():
        ext_ref[0:SUBLANES, :] = jnp.zeros((SUBLANES, 3 * width), F32)
        state_ref[...] = jnp.zeros(state_ref.shape, F32)

    @pl.when(i > 0)
    def _():
        ext_ref[0:SUBLANES, :] = ext_ref[ts:ts + SUBLANES, :]

    ext_ref[SUBLANES:SUBLANES + ts, :] = qkv_ref[...]

    for cb in range(3 * nh):
        cols = slice(cb * dh, (cb + 1) * dh)
        acc = None
        for k in range(SHORT_CONV):
            lo = SUBLANES - (SHORT_CONV - 1) + k
            term = cw_ref[k:k + 1, cols] * ext_ref[lo:lo + ts, cols]
            acc = term if acc is None else acc + term
        z = _silu(acc)
        if cb < 2 * nh:
            z = z * lax.rsqrt(jnp.sum(z * z, axis=-1, keepdims=True) + RMS_EPS)
            if cb < nh:
                z = z * (dh ** -0.5)
        act_ref[:, cols] = z

    ii = lax.broadcasted_iota(I32, (CHUNK, CHUNK), 0)
    jj = lax.broadcasted_iota(I32, (CHUNK, CHUNK), 1)
    ltri = ltri_ref[...]

    def prep_body(cp, carry):
        r0 = pl.multiple_of(cp * (GDN_GROUP * CHUNK), GDN_GROUP * CHUNK)
        ba = ba_ref[pl.ds(r0, GDN_GROUP * CHUNK), :]
        beta = _sigmoid(ba)
        xg = ba + dtb_ref[...]
        softplus = jnp.maximum(xg, 0.0) + jnp.log1p(jnp.exp(-jnp.abs(xg)))
        g = -jnp.exp(alog_ref[...]) * softplus
        g1 = g.astype(BF16)
        r1 = g - g1.astype(F32)
        g2 = r1.astype(BF16)
        g3 = (r1 - g2.astype(F32)).astype(BF16)
        gc = _dot(ltri, g1) + _dot(ltri, g2) + _dot(ltri, g3)
        gc_t = gc.T
        egc = jnp.exp(gc)

        pairs = [(c, h) for c in range(GDN_GROUP) for h in range(nh)]
        rows_of = lambda c: slice(c * CHUNK, (c + 1) * CHUNK)
        col = lambda arr, lane: jnp.stack([arr[rows_of(c), lane(h):lane(h) + 1] for c, h in pairs])
        beta_c = col(beta, lambda h: h)
        gc_c = col(gc, lambda h: nh + h)
        egc_c = col(egc, lambda h: nh + h)
        gc_r = jnp.stack([gc_t[nh + h:nh + h + 1, rows_of(c)] for c, h in pairs])
        g_last = jnp.stack([gc[(c + 1) * CHUNK - 1:(c + 1) * CHUNK, nh + h:nh + h + 1] for c, h in pairs])
        act = lambda part: jnp.stack([act_ref[pl.ds(r0 + c * CHUNK, CHUNK),
                                              part * width + h * dh:part * width + (h + 1) * dh] for c, h in pairs])
        q, k, v = act(0), act(1), act(2)

        dec = jnp.exp(jnp.where(ii >= jj, gc_c - gc_r, 0.0))
        kb = k * beta_c
        k_bf = k.astype(BF16)
        a = _bmm_nt(kb.astype(BF16), k_bf) * jnp.where(ii > jj, dec, 0.0)
        attn = _bmm_nt(q.astype(BF16), k_bf) * jnp.where(ii >= jj, dec, 0.0)
        tm1 = _unit_lower_inverse_minus_eye(a, ii, jj).astype(BF16)
        vb = v * beta_c
        kbg = kb * egc_c
        n0 = pl.multiple_of(cp * (GDN_GROUP * nh), GDN_GROUP * nh)
        sl = pl.ds(n0, GDN_GROUP * nh)
        u_ref[sl] = vb + _bmm(tm1, vb.astype(BF16))
        w_ref[sl] = (kbg + _bmm(tm1, kbg.astype(BF16))).astype(BF16)
        attn_ref[sl] = attn.astype(BF16)
        qg_ref[sl] = (q * egc_c).astype(BF16)
        kdt_ref[sl] = jnp.swapaxes(k * jnp.exp(g_last - gc_c), 1, 2).astype(BF16)
        el_ref[sl] = jnp.broadcast_to(jnp.exp(g_last), (GDN_GROUP * nh, 1, dh))
        return carry

    lax.fori_loop(jnp.int32(0), jnp.int32(ts // (GDN_GROUP * CHUNK)), prep_body, 0)

    def chunk_body(c, carry):
        r0 = pl.multiple_of(c * CHUNK, CHUNK)
        sl = pl.ds(pl.multiple_of(c * nh, nh), nh)
        st = state_ref[...]
        st_bf = st.astype(BF16)
        v_new = u_ref[sl] - _bmm(w_ref[sl], st_bf)
        vn_bf = v_new.astype(BF16)
        o = _bmm(qg_ref[sl], st_bf) + _bmm(attn_ref[sl], vn_bf)
        state_ref[...] = st * el_ref[sl] + _bmm(kdt_ref[sl], vn_bf)
        on = o * lax.rsqrt(jnp.mean(o * o, axis=-1, keepdims=True) + RMS_EPS) * onorm_ref[...]
        for h in range(nh):
            hc = slice(h * dh, (h + 1) * dh)
            y_ref[pl.ds(r0, CHUNK), hc] = (on[h] * _silu(gate_ref[pl.ds(r0, CHUNK), hc])).astype(BF16)
        return carry

    lax.fori_loop(jnp.int32(0), jnp.int32(ts // CHUNK), chunk_body, 0)


def _gdn(qkv, gate, ba, conv_w, a_log, dt_bias, out_norm):
    b, s, w3 = qkv.shape
    width = w3 // 3
    dh = width // GDN_HEADS
    ts = min(512, s)
    pad_a = lambda v: jnp.pad(v.reshape(1, GDN_HEADS), ((0, 0), (GDN_HEADS, LANES - 2 * GDN_HEADS)))
    gc_rows = GDN_GROUP * CHUNK
    ri = lax.broadcasted_iota(I32, (gc_rows, gc_rows), 0)
    ci = lax.broadcasted_iota(I32, (gc_rows, gc_rows), 1)
    ltri = ((ri >= ci) & (ri // CHUNK == ci // CHUNK)).astype(BF16)
    n_mat = (ts // CHUNK) * GDN_HEADS
    kern = functools.partial(_gdn_kernel, ts=ts, width=width, dh=dh)
    row = lambda c: pl.BlockSpec((None, ts, c), lambda i, j: (i, j, 0))
    full = lambda shape: pl.BlockSpec(shape, lambda i, j: (0,) * len(shape))
    return pl.pallas_call(
        kern,
        grid=(b, s // ts),
        in_specs=[row(w3), row(width), row(LANES), full(conv_w.shape), full((1, LANES)), full((1, LANES)),
                  full((1, dh)), full((gc_rows, gc_rows))],
        out_specs=row(width),
        out_shape=jax.ShapeDtypeStruct((b, s, width), BF16),
        scratch_shapes=[pltpu.VMEM((ts + SUBLANES, w3), F32),
                        pltpu.VMEM((ts, w3), F32),
                        pltpu.VMEM((GDN_HEADS, dh, dh), F32),
                        pltpu.VMEM((n_mat, CHUNK, dh), F32),
                        pltpu.VMEM((n_mat, CHUNK, dh), BF16),
                        pltpu.VMEM((n_mat, CHUNK, CHUNK), BF16),
                        pltpu.VMEM((n_mat, CHUNK, dh), BF16),
                        pltpu.VMEM((n_mat, dh, CHUNK), BF16),
                        pltpu.VMEM((n_mat, 1, dh), F32)],
        compiler_params=_params("arbitrary", "arbitrary"),
    )(qkv, gate, ba, conv_w, pad_a(a_log), pad_a(dt_bias), out_norm.reshape(1, dh), ltri)


def kernel(x, c, e_mod_w, e_mod_b, e_w_in, e_pool_w, e_pool_scale, e_kv_norm, e_w_uk, e_w_uv, e_w_out, e_ln_g, e_ln_b, o_mod_w, o_mod_b, o_w_in, o_conv_w, o_a_log, o_dt_bias, o_out_norm, o_w_out, o_ln_g, o_ln_b, f_mod_w, f_mod_b, f_w_up, f_conv_w, f_conv_b, f_w_down, f_ln_g, f_ln_b):
    depth = f_mod_w.shape[0]
    alpha = (2 * depth) ** 0.25
    for layer in range(depth):
        i = layer // 2
        if layer % 2 == 0:
            mods = _modulation(c, e_mod_w, e_mod_b, i)
            u, ql, ckv, qi, kw = _even_in_proj(x, mods, e_w_in[i], e_kv_norm[i], e_w_uk[i])
            y_pool = _pool_mixer(u, e_pool_w[i], e_pool_scale[i])
            y_dsa = _dsa_attention(ql, qi, kw, ckv, e_w_uv[i])
            x = _out_proj_ln(x, mods, [y_pool, y_dsa], e_w_out[i], e_ln_g[i], e_ln_b[i], alpha)
        else:
            mods = _modulation(c, o_mod_w, o_mod_b, i)
            width = o_w_out.shape[1]
            qkv, gate, ba = _odd_in_proj(x, mods, o_w_in[i], width)
            y = _gdn(qkv, gate, ba, o_conv_w[i], o_a_log[i], o_dt_bias[i], o_out_norm[i])
            x = _out_proj_ln(x, mods, [y], o_w_out[i], o_ln_g[i], o_ln_b[i], alpha)
        mods = _modulation(c, f_mod_w, f_mod_b, layer)
        act = _ffn_up(x, mods, f_w_up[layer], f_conv_w[layer], f_conv_b[layer])
        x = _out_proj_ln(x, mods, [act], f_w_down[layer], f_ln_g[layer], f_ln_b[layer], alpha)
    return x
```

```python
import functools
import math

import jax
import jax.numpy as jnp
from jax import lax
from jax.experimental import pallas as pl
from jax.experimental.pallas import tpu as pltpu

F32 = jnp.float32
BF16 = jnp.bfloat16
I32 = jnp.int32

LANES = 128
SUBLANES = 8
VMEM_LIMIT = 56 * 1024 * 1024

CHUNK = 64
POOL_WINDOWS = (2, 4, 8, 16)
POOL_HALO = 16
DSA_HEADS = 8
DSA_HEAD_DIM = 64
IDX_HEADS = 4
IDX_DIM = 64
TOPK_MAX = 256
GDN_HEADS = 8
GDN_GROUP = 2
SHORT_CONV = 4
FFN_CONV = 3
LN_EPS = 1e-5
RMS_EPS = 1e-6
NEG_BIG = -1e30
INT_MIN = -(2 ** 31)


def _dot(a, b):
    return jnp.dot(a, b, preferred_element_type=F32)


def _dot_nt(a, b):
    return lax.dot_general(a, b, (((1,), (1,)), ((), ())), preferred_element_type=F32)


def _dot_tn(a, b):
    return lax.dot_general(a, b, (((0,), (0,)), ((), ())), preferred_element_type=F32)


def _split(a):
    hi = a.astype(BF16)
    lo = (a - hi.astype(F32)).astype(BF16)
    return hi, lo


def _sigmoid(x):
    return 1.0 / (1.0 + jnp.exp(-x))


def _silu(x):
    return x * _sigmoid(x)


def _layer_norm(z, g, b):
    mu = jnp.mean(z, axis=-1, keepdims=True)
    zc = z - mu
    var = jnp.mean(zc * zc, axis=-1, keepdims=True)
    return zc * lax.rsqrt(var + LN_EPS) * g + b


def _params(*sem):
    return pltpu.CompilerParams(dimension_semantics=sem, vmem_limit_bytes=VMEM_LIMIT)


def _mod_kernel(c_ref, w_ref, b_ref, o_ref):
    c = c_ref[...]
    a_hi, a_lo = _split(_silu(c))
    w_hi, w_lo = _split(w_ref[...])
    o_ref[...] = _dot(a_hi, w_hi) + _dot(a_hi, w_lo) + _dot(a_lo, w_hi) + b_ref[...]


def _modulation(c, mod_w, mod_b, layer):
    b, d = c.shape
    n3 = mod_w.shape[-1]
    tn = 512
    bias = mod_b.reshape(mod_b.shape[0], 1, n3)
    out = pl.pallas_call(
        _mod_kernel,
        grid=(n3 // tn,),
        in_specs=[pl.BlockSpec((b, d), lambda j: (0, 0)),
                  pl.BlockSpec((None, d, tn), lambda j: (layer, 0, j)),
                  pl.BlockSpec((None, 1, tn), lambda j: (layer, 0, j))],
        out_specs=pl.BlockSpec((b, tn), lambda j: (0, j)),
        out_shape=jax.ShapeDtypeStruct((b, n3), F32),
        compiler_params=_params("arbitrary"),
    )(c, mod_w, bias)
    return out.reshape(b, 3, d)


def _modulate(x, mod):
    return x * (1.0 + mod[1:2, :]) + mod[0:1, :]


def _even_in_kernel(x_ref, mod_ref, wm_ref, wih_ref, wil_ref, kvn_ref, wuk_ref,
                    u_ref, ql_ref, ckv_ref, qi_ref, kw_ref, *, pool_w, dsa_w, kv_rank):
    h = _modulate(x_ref[...], mod_ref[...])
    h_hi, h_lo = _split(h)
    main = _dot(h_hi, wm_ref[...])
    u_ref[...] = main[:, :pool_w]
    q = main[:, pool_w:pool_w + dsa_w]
    ql_ref[...] = _dot(q.astype(BF16), wuk_ref[...]).astype(BF16)
    ckv = main[:, pool_w + dsa_w:]
    ms = jnp.mean(ckv * ckv, axis=-1, keepdims=True)
    ckv_ref[...] = (ckv * lax.rsqrt(ms + RMS_EPS) * kvn_ref[...]).astype(BF16)
    idx = _dot(h_hi, wih_ref[...]) + _dot(h_hi, wil_ref[...]) + _dot(h_lo, wih_ref[...])
    nqi = IDX_HEADS * IDX_DIM
    qi_ref[...] = idx[:, :nqi]
    kw_ref[...] = idx[:, nqi:]


def _even_in_proj(x, mods, w_in, kv_norm, w_uk):
    b, s, d = x.shape
    pool_w = len(POOL_WINDOWS) * LANES
    dsa_w = DSA_HEADS * DSA_HEAD_DIM
    kv_rank = w_uk.shape[1]
    n_main = pool_w + dsa_w + kv_rank
    nqi = IDX_HEADS * IDX_DIM
    n_idx = nqi + LANES
    w_main = w_in[:, :n_main].astype(BF16)
    w_idx = jnp.pad(w_in[:, n_main:], ((0, 0), (0, n_main + n_idx - w_in.shape[1])))
    wih, wil = _split(w_idx)
    wuk_t = jnp.swapaxes(w_uk, 1, 2) * (DSA_HEAD_DIM ** -0.5)
    eye = jnp.eye(DSA_HEADS, dtype=F32)
    wuk_bd = (eye[:, None, :, None] * wuk_t[:, :, None, :]).reshape(dsa_w, DSA_HEADS * kv_rank).astype(BF16)
    tm = min(512, s)
    row = lambda c: pl.BlockSpec((None, tm, c), lambda i, j: (i, j, 0))
    full = lambda a: pl.BlockSpec(a.shape, lambda i, j: (0,) * a.ndim)
    kvn = kv_norm.reshape(1, kv_rank)
    kern = functools.partial(_even_in_kernel, pool_w=pool_w, dsa_w=dsa_w, kv_rank=kv_rank)
    return pl.pallas_call(
        kern,
        grid=(b, s // tm),
        in_specs=[row(d), pl.BlockSpec((None, 3, d), lambda i, j: (i, 0, 0)),
                  full(w_main), full(wih), full(wil), full(kvn), full(wuk_bd)],
        out_specs=[row(pool_w), row(DSA_HEADS * kv_rank), row(kv_rank), row(nqi), row(LANES)],
        out_shape=[jax.ShapeDtypeStruct((b, s, pool_w), F32),
                   jax.ShapeDtypeStruct((b, s, DSA_HEADS * kv_rank), BF16),
                   jax.ShapeDtypeStruct((b, s, kv_rank), BF16),
                   jax.ShapeDtypeStruct((b, s, nqi), F32),
                   jax.ShapeDtypeStruct((b, s, LANES), F32)],
        compiler_params=_params("arbitrary", "arbitrary"),
    )(x, mods, w_main, wih, wil, kvn, wuk_bd)


def _pool_kernel(u_ref, halo_ref, w_ref, scale_ref, o_ref, ext_ref, *, ts):
    i = pl.program_id(1)
    ext_ref[0:POOL_HALO, :] = jnp.where(i > 0, halo_ref[...], 0.0)
    ext_ref[POOL_HALO:POOL_HALO + ts, :] = u_ref[...]
    t = i * ts + lax.broadcasted_iota(I32, (ts, 1), 0)
    for g, win in enumerate(POOL_WINDOWS):
        cols = slice(g * LANES, (g + 1) * LANES)
        cur = ext_ref[POOL_HALO:POOL_HALO + ts, cols]
        acc = cur
        for j in range(1, win):
            acc = acc + ext_ref[POOL_HALO - j:POOL_HALO - j + ts, cols]
        cnt = jnp.minimum(t + 1, win).astype(F32)
        pooled = acc / cnt - cur
        y = _dot(pooled.astype(BF16), w_ref[g]) * scale_ref[:, cols]
        o_ref[:, cols] = y.astype(BF16)


def _pool_mixer(u, pool_w, pool_scale):
    b, s, width = u.shape
    ts = min(512, s)
    kern = functools.partial(_pool_kernel, ts=ts)
    hb = ts // POOL_HALO
    return pl.pallas_call(
        kern,
        grid=(b, s // ts),
        in_specs=[pl.BlockSpec((None, ts, width), lambda i, j: (i, j, 0)),
                  pl.BlockSpec((None, POOL_HALO, width), lambda i, j: (i, jnp.maximum(j * hb - 1, 0), 0)),
                  pl.BlockSpec(pool_w.shape, lambda i, j: (0, 0, 0)),
                  pl.BlockSpec((1, width), lambda i, j: (0, 0))],
        out_specs=pl.BlockSpec((None, ts, width), lambda i, j: (i, j, 0)),
        out_shape=jax.ShapeDtypeStruct((b, s, width), BF16),
        scratch_shapes=[pltpu.VMEM((ts + POOL_HALO, width), F32)],
        compiler_params=_params("arbitrary", "arbitrary"),
    )(u, u, pool_w.astype(BF16), pool_scale.reshape(1, width))


def _dsa_kernel(ql_ref, qi_ref, kwq_ref, kw_ref, ckv_ref, tri_ref, wuv_ref, o_ref,
                ki3_ref, keys_ref, qs_ref, m_ref, l_ref, acc_ref, *, t, sk, topk, rank):
    qb = pl.program_id(1)
    s = kw_ref.shape[0]
    lane = lax.broadcasted_iota(I32, (1, LANES), 1)
    low = lane < IDX_DIM

    @pl.when(qb == 0)
    def _():
        k_hi, k_lo = _split(jnp.where(low, kw_ref[...], 0.0))
        hi_f = k_hi.astype(F32)
        ki3_ref[:, 0:LANES] = (hi_f + pltpu.roll(hi_f, IDX_DIM, 1)).astype(BF16)
        ki3_ref[:, LANES:2 * LANES] = k_lo

    parts = []
    for h in range(IDX_HEADS):
        xq = qi_ref[:, (h // 2) * LANES:(h // 2 + 1) * LANES]
        if h % 2:
            xq = pltpu.roll(xq, IDX_DIM, 1)
        q_hi, q_lo = _split(jnp.where(low, xq, 0.0))
        hi_f = q_hi.astype(F32)
        a = (hi_f + pltpu.roll(q_lo.astype(F32), IDX_DIM, 1)).astype(BF16)
        parts.append(jnp.concatenate([a, q_hi], axis=1))
    qi3 = jnp.concatenate(parts, axis=0)

    wscale = IDX_HEADS ** -0.5 * IDX_DIM ** -0.5
    wi = [kwq_ref[:, IDX_DIM + h:IDX_DIM + h + 1] * wscale for h in range(IDX_HEADS)]

    row = lax.broadcasted_iota(I32, (t, 1), 0)
    adm = qb * t + (row // CHUNK + 1) * CHUNK
    nkb = (qb * t + t + sk - 1) // sk
    col0 = lax.broadcasted_iota(I32, (1, sk), 1)

    def score_body(j, carry):
        off = pl.multiple_of(j * sk, sk)
        sc = _dot_nt(qi3, ki3_ref[pl.ds(off, sk), :])
        score = jnp.zeros((t, sk), F32)
        for h in range(IDX_HEADS):
            score = score + wi[h] * jnp.maximum(sc[h * t:(h + 1) * t, :], 0.0)
        score = jnp.where(score == 0.0, 0.0, score)
        bits = lax.bitcast_convert_type(score, I32)
        key = bits ^ (lax.shift_right_arithmetic(bits, jnp.int32(31)) & jnp.int32(0x7FFFFFFF))
        keys_ref[:, pl.ds(off, sk)] = jnp.where(col0 + off < adm, key, jnp.int32(INT_MIN))
        return carry

    i0 = jnp.int32(0)
    lax.fori_loop(i0, nkb, score_body, 0)

    def count(pred, ref):
        ref_b = jnp.broadcast_to(ref, (t, LANES))

        def body(j, acc):
            off = pl.multiple_of(j * sk, sk)
            for c in range(sk // LANES):
                hit = pred(keys_ref[:, pl.ds(pl.multiple_of(off + c * LANES, LANES), LANES)], ref_b)
                acc = acc + jnp.where(hit, 1.0, 0.0)
            return acc
        acc = lax.fori_loop(i0, nkb, body, jnp.zeros((t, LANES), F32))
        return jnp.sum(acc, axis=1, keepdims=True)

    ge = lambda k, r: k >= r
    kf = float(topk)
    zero = jnp.zeros((t, 1), I32)
    res0 = jnp.where(count(ge, zero) >= kf, zero, jnp.int32(INT_MIN))

    def bit_body(i, res):
        cand = res | lax.shift_left(jnp.int32(1), jnp.int32(30) - i)
        return jnp.where(count(ge, cand) >= kf, cand, res)

    kth = lax.fori_loop(i0, jnp.int32(31), bit_body, res0)
    need = kf - count(lambda k, r: k > r, kth)

    for h in range(DSA_HEADS):
        qs_ref[h * t:(h + 1) * t, :] = ql_ref[:, h * rank:(h + 1) * rank]
    m_ref[...] = jnp.full(m_ref.shape, NEG_BIG, F32)
    l_ref[...] = jnp.zeros(l_ref.shape, F32)
    acc_ref[...] = jnp.zeros(acc_ref.shape, F32)

    def attn_body(j, seen):
        off = pl.multiple_of(j * sk, sk)
        key = keys_ref[:, pl.ds(off, sk)]
        eq = key == kth
        rank_eq = seen + _dot(eq.astype(BF16), tri_ref[...])
        sel = ((key > kth) | (eq & (rank_eq <= need))) & (col0 + off < adm)
        kv = ckv_ref[pl.ds(off, sk), :]
        logits = _dot_nt(qs_ref[...], kv).reshape(DSA_HEADS, t, sk)
        logits = jnp.where(sel[None], logits, NEG_BIG).reshape(DSA_HEADS * t, sk)
        m_old = m_ref[...]
        m_new = jnp.maximum(m_old, jnp.max(logits, axis=1, keepdims=True))
        alpha = jnp.exp(m_old - m_new)
        p = jnp.exp(logits - m_new)
        l_ref[...] = alpha * l_ref[...] + jnp.sum(p, axis=1, keepdims=True)
        acc_ref[...] = alpha * acc_ref[...] + _dot(p.astype(BF16), kv)
        m_ref[...] = m_new
        return seen + jnp.sum(eq.astype(F32), axis=1, keepdims=True)

    lax.fori_loop(i0, nkb, attn_body, jnp.zeros((t, 1), F32))

    o = acc_ref[...] / l_ref[...]
    y = jnp.zeros(o_ref.shape, F32)
    for h in range(DSA_HEADS):
        y = y + _dot(o[h * t:(h + 1) * t, :].astype(BF16), wuv_ref[h])
    o_ref[...] = y.astype(BF16)


def _dsa_attention(ql, qi, kw, ckv, w_uv):
    b, s, _ = ql.shape
    rank = ckv.shape[-1]
    t = 2 * CHUNK
    sk = min(512, s)
    topk = min(TOPK_MAX, s // 4)
    width = DSA_HEADS * DSA_HEAD_DIM
    tri = (lax.broadcasted_iota(I32, (sk, sk), 0) <= lax.broadcasted_iota(I32, (sk, sk), 1)).astype(BF16)
    eye = jnp.eye(DSA_HEADS, dtype=F32)
    wuv = (w_uv[:, :, None, :] * eye[:, None, :, None]).reshape(DSA_HEADS, rank, width).astype(BF16)
    kern = functools.partial(_dsa_kernel, t=t, sk=sk, topk=topk, rank=rank)
    blk = lambda c: pl.BlockSpec((None, t, c), lambda i, j: (i, j, 0))
    seq = lambda c: pl.BlockSpec((None, s, c), lambda i, j: (i, 0, 0))
    return pl.pallas_call(
        kern,
        grid=(b, s // t),
        in_specs=[blk(DSA_HEADS * rank), blk(qi.shape[-1]), blk(LANES), seq(LANES), seq(rank),
                  pl.BlockSpec((sk, sk), lambda i, j: (0, 0)),
                  pl.BlockSpec(wuv.shape, lambda i, j: (0, 0, 0))],
        out_specs=blk(width),
        out_shape=jax.ShapeDtypeStruct((b, s, width), BF16),
        scratch_shapes=[pltpu.VMEM((s, 2 * LANES), BF16),
                        pltpu.VMEM((t, s), I32),
                        pltpu.VMEM((DSA_HEADS * t, rank), BF16),
                        pltpu.VMEM((DSA_HEADS * t, 1), F32),
                        pltpu.VMEM((DSA_HEADS * t, 1), F32),
                        pltpu.VMEM((DSA_HEADS * t, rank), F32)],
        compiler_params=_params("arbitrary", "arbitrary"),
    )(ql, qi, kw, kw, ckv, tri, wuv)


LOG2E = 1.4426950408889634


def _dsat_kernel(ql_ref, qi_ref, kwq_ref, kw_ref, ckv_ref, tri_ref, wuv_ref, o_ref,
                 ki3_ref, kvt_ref, keys_ref, qlt_ref, m_ref, l_ref, acc_ref, *, t, sk, topk, rank):
    qb = pl.program_id(1)
    lane = lax.broadcasted_iota(I32, (1, LANES), 1)
    low = lane < IDX_DIM
    nacc = 4 * SUBLANES

    @pl.when(qb == 0)
    def _():
        k_hi, k_lo = _split(jnp.where(low, kw_ref[...], 0.0))
        hi_f = k_hi.astype(F32)
        ki3_ref[:, 0:LANES] = (hi_f + pltpu.roll(hi_f, IDX_DIM, 1)).astype(BF16)
        ki3_ref[:, LANES:2 * LANES] = k_lo
        kvt_ref[...] = ckv_ref[...].astype(F32).T.astype(BF16)

    parts = []
    for h in range(IDX_HEADS):
        xq = qi_ref[:, (h // 2) * LANES:(h // 2 + 1) * LANES]
        if h % 2:
            xq = pltpu.roll(xq, IDX_DIM, 1)
        q_hi, q_lo = _split(jnp.where(low, xq, 0.0))
        hi_f = q_hi.astype(F32)
        a = (hi_f + pltpu.roll(q_lo.astype(F32), IDX_DIM, 1)).astype(BF16)
        parts.append(jnp.concatenate([a, q_hi], axis=1))
    qi3 = jnp.concatenate(parts, axis=0)

    kwq_t = kwq_ref[...].T
    wscale = IDX_HEADS ** -0.5 * IDX_DIM ** -0.5
    wi = [kwq_t[IDX_DIM + h:IDX_DIM + h + 1, :] * wscale for h in range(IDX_HEADS)]

    for h in range(DSA_HEADS):
        qh = ql_ref[:, h * rank:(h + 1) * rank].astype(F32).T * LOG2E
        qlt_ref[:, h * t:(h + 1) * t] = qh.astype(BF16)

    qcol = lax.broadcasted_iota(I32, (1, t), 1)
    adm = qb * t + (qcol // CHUNK + 1) * CHUNK
    nkb = (qb * t + t + sk - 1) // sk
    krow = lax.broadcasted_iota(I32, (sk, 1), 0)
    i0 = jnp.int32(0)

    def score_body(j, carry):
        off = pl.multiple_of(j * sk, sk)
        sc = _dot_nt(ki3_ref[pl.ds(off, sk), :], qi3)
        score = jnp.zeros((sk, t), F32)
        for h in range(IDX_HEADS):
            score = score + wi[h] * jnp.maximum(sc[:, h * t:(h + 1) * t], 0.0)
        score = jnp.where(score == 0.0, 0.0, score)
        bits = lax.bitcast_convert_type(score, I32)
        key = bits ^ (lax.shift_right_arithmetic(bits, jnp.int32(31)) & jnp.int32(0x7FFFFFFF))
        keys_ref[pl.ds(off, sk), :] = jnp.where(krow + off < adm, key, jnp.int32(INT_MIN))
        return carry

    lax.fori_loop(i0, nkb, score_body, 0)

    def count(pred, ref):
        def body(j, acc):
            off = pl.multiple_of(j * sk, sk)
            hit = pred(keys_ref[pl.ds(off, sk), :], ref).astype(F32)
            return acc + jnp.sum(hit.reshape(sk // nacc, nacc, t), axis=0)
        acc = lax.fori_loop(i0, nkb, body, jnp.zeros((nacc, t), F32))
        return jnp.sum(acc, axis=0, keepdims=True)

    ge = lambda k, r: k >= r
    kf = float(topk)
    zero = jnp.zeros((1, t), I32)
    res0 = jnp.where(count(ge, zero) >= kf, zero, jnp.int32(INT_MIN))

    def bit_body(i, res):
        cand = res | lax.shift_left(jnp.int32(1), jnp.int32(30) - i)
        return jnp.where(count(ge, cand) >= kf, cand, res)

    kth = lax.fori_loop(i0, jnp.int32(31), bit_body, res0)
    need = kf - count(lambda k, r: k > r, kth)

    m_ref[...] = jnp.full(m_ref.shape, NEG_BIG, F32)
    l_ref[...] = jnp.zeros(l_ref.shape, F32)
    acc_ref[...] = jnp.zeros(acc_ref.shape, F32)

    def attn_body(j, seen):
        off = pl.multiple_of(j * sk, sk)
        key = keys_ref[pl.ds(off, sk), :]
        eq = key == kth
        eq_f = eq.astype(F32)
        rank_eq = seen + _dot(tri_ref[...], eq_f.astype(BF16))
        sel = ((key > kth) | (eq & (rank_eq <= need))) & (krow + off < adm)
        bias = jnp.where(sel, jnp.float32(0.0), jnp.float32(NEG_BIG))
        logits = _dot(ckv_ref[pl.ds(off, sk), :], qlt_ref[...])
        kvt = kvt_ref[:, pl.ds(off, sk)]
        for h in range(DSA_HEADS):
            lg = logits[:, h * t:(h + 1) * t] + bias
            m_old = m_ref[h:h + 1, :]
            m_new = jnp.maximum(m_old, jnp.max(lg, axis=0, keepdims=True))
            alpha = jnp.exp2(m_old - m_new)
            p = jnp.exp2(lg - m_new)
            l_ref[h:h + 1, :] = alpha * l_ref[h:h + 1, :] + jnp.sum(p, axis=0, keepdims=True)
            acc_ref[h] = alpha * acc_ref[h] + _dot(kvt, p.astype(BF16))
            m_ref[h:h + 1, :] = m_new
        return seen + jnp.sum(eq_f, axis=0, keepdims=True)

    lax.fori_loop(i0, nkb, attn_body, jnp.zeros((1, t), F32))

    y_t = jnp.zeros((o_ref.shape[1], t), F32)
    for h in range(DSA_HEADS):
        o_t = acc_ref[h] / l_ref[h:h + 1, :]
        y_t = y_t + _dot(wuv_ref[h], o_t.astype(BF16))
    o_ref[...] = y_t.T.astype(BF16)


def _dsa_attention_t(ql, qi, kw, ckv, w_uv):
    b, s, _ = ql.shape
    rank = ckv.shape[-1]
    t = 2 * CHUNK
    sk = min(512, s)
    topk = min(TOPK_MAX, s // 4)
    width = DSA_HEADS * DSA_HEAD_DIM
    tri = (lax.broadcasted_iota(I32, (sk, sk), 0) >= lax.broadcasted_iota(I32, (sk, sk), 1)).astype(BF16)
    eye = jnp.eye(DSA_HEADS, dtype=F32)
    wuv_t = jnp.swapaxes(w_uv, 1, 2)
    wuv = (eye[:, :, None, None] * wuv_t[:, None, :, :]).reshape(DSA_HEADS, width, rank).astype(BF16)
    kern = functools.partial(_dsat_kernel, t=t, sk=sk, topk=topk, rank=rank)
    blk = lambda c: pl.BlockSpec((None, t, c), lambda i, j: (i, j, 0))
    seq = lambda c: pl.BlockSpec((None, s, c), lambda i, j: (i, 0, 0))
    return pl.pallas_call(
        kern,
        grid=(b, s // t),
        in_specs=[blk(DSA_HEADS * rank), blk(qi.shape[-1]), blk(LANES), seq(LANES), seq(rank),
                  pl.BlockSpec((sk, sk), lambda i, j: (0, 0)),
                  pl.BlockSpec(wuv.shape, lambda i, j: (0, 0, 0))],
        out_specs=blk(width),
        out_shape=jax.ShapeDtypeStruct((b, s, width), BF16),
        scratch_shapes=[pltpu.VMEM((s, 2 * LANES), BF16),
                        pltpu.VMEM((rank, s), BF16),
                        pltpu.VMEM((s, t), I32),
                        pltpu.VMEM((rank, DSA_HEADS * t), BF16),
                        pltpu.VMEM((DSA_HEADS, t), F32),
                        pltpu.VMEM((DSA_HEADS, t), F32),
                        pltpu.VMEM((DSA_HEADS, rank, t), F32)],
        compiler_params=_params("arbitrary", "arbitrary"),
    )(ql, qi, kw, kw, ckv, tri, wuv)


def _resln_kernel(*refs, n_y, widths, alpha):
    x_ref, mod_ref = refs[0], refs[1]
    y_refs = refs[2:2 + n_y]
    w_ref, g_ref, b_ref, o_ref = refs[2 + n_y:]
    acc = None
    off = 0
    for y_ref, wd in zip(y_refs, widths):
        part = _dot(y_ref[...], w_ref[off:off + wd, :])
        acc = part if acc is None else acc + part
        off += wd
    z = alpha * x_ref[...] + mod_ref[2:3, :] * acc
    o_ref[...] = _layer_norm(z, g_ref[...], b_ref[...])


def _out_proj_ln(x, mods, ys, w, ln_g, ln_b, alpha):
    b, s, d = x.shape
    widths = tuple(y.shape[-1] for y in ys)
    tm = min(512, s)
    kern = functools.partial(_resln_kernel, n_y=len(ys), widths=widths, alpha=alpha)
    row = lambda c: pl.BlockSpec((None, tm, c), lambda i, j: (i, j, 0))
    vec = pl.BlockSpec((1, d), lambda i, j: (0, 0))
    return pl.pallas_call(
        kern,
        grid=(b, s // tm),
        in_specs=[row(d), pl.BlockSpec((None, 3, d), lambda i, j: (i, 0, 0))]
                 + [row(wd) for wd in widths]
                 + [pl.BlockSpec(w.shape, lambda i, j: (0, 0)), vec, vec],
        out_specs=row(d),
        out_shape=jax.ShapeDtypeStruct((b, s, d), F32),
        compiler_params=_params("arbitrary", "arbitrary"),
    )(x, mods, *ys, w.astype(BF16), ln_g.reshape(1, d), ln_b.reshape(1, d))


def _ffn_up_kernel(x_ref, halo_ref, mod_ref, wa_ref, wv_ref, cwa_ref, cwv_ref, cba_ref, cbv_ref,
                   o_ref, h_ref, up_ref, *, tm):
    i = pl.program_id(1)
    j = pl.program_id(2)

    @pl.when(j == 0)
    def _():
        mod = mod_ref[...]
        halo = jnp.where(i > 0, _modulate(halo_ref[...], mod), 0.0)
        h_ref[0:SUBLANES, :] = halo.astype(BF16)
        h_ref[SUBLANES:SUBLANES + tm, :] = _modulate(x_ref[...], mod).astype(BF16)

    def conv(w_ref, cw_ref, cb_ref):
        up_ref[...] = _dot(h_ref[...], w_ref[...])
        out = cb_ref[...]
        for k in range(FFN_CONV):
            lo = SUBLANES - (FFN_CONV - 1) + k
            out = out + cw_ref[k:k + 1, :] * up_ref[lo:lo + tm, :]
        return out

    a = conv(wa_ref, cwa_ref, cba_ref)
    v = conv(wv_ref, cwv_ref, cbv_ref)
    o_ref[...] = (_silu(a) * v).astype(BF16)


def _ffn_up(x, mods, w_up, conv_w, conv_b):
    b, s, d = x.shape
    d_ff = w_up.shape[1] // 2
    tm = min(512, s)
    tn = d_ff // 3 if d_ff % (3 * LANES) == 0 else d_ff
    ncol = d_ff // tn
    wb = w_up.astype(BF16)
    cb = conv_b.reshape(1, 2 * d_ff)
    hb = tm // SUBLANES
    kern = functools.partial(_ffn_up_kernel, tm=tm)
    col_a = lambda r: pl.BlockSpec((r, tn), lambda i, j, k: (0, k))
    col_v = lambda r: pl.BlockSpec((r, tn), lambda i, j, k: (0, k + ncol))
    return pl.pallas_call(
        kern,
        grid=(b, s // tm, ncol),
        in_specs=[pl.BlockSpec((None, tm, d), lambda i, j, k: (i, j, 0)),
                  pl.BlockSpec((None, SUBLANES, d), lambda i, j, k: (i, jnp.maximum(j * hb - 1, 0), 0)),
                  pl.BlockSpec((None, 3, d), lambda i, j, k: (i, 0, 0)),
                  col_a(d), col_v(d), col_a(FFN_CONV), col_v(FFN_CONV), col_a(1), col_v(1)],
        out_specs=pl.BlockSpec((None, tm, tn), lambda i, j, k: (i, j, k)),
        out_shape=jax.ShapeDtypeStruct((b, s, d_ff), BF16),
        scratch_shapes=[pltpu.VMEM((tm + SUBLANES, d), BF16),
                        pltpu.VMEM((tm + SUBLANES, tn), F32)],
        compiler_params=_params("arbitrary", "arbitrary", "arbitrary"),
    )(x, x, mods, wb, wb, conv_w, conv_w, cb, cb)


def _odd_in_kernel(x_ref, mod_ref, wq_ref, wg_ref, wbh_ref, wbl_ref, qkv_ref, gate_ref, ba_ref):
    h = _modulate(x_ref[...], mod_ref[...])
    h_hi, h_lo = _split(h)
    qkv_ref[...] = _dot(h_hi, wq_ref[...])
    gate_ref[...] = _dot(h_hi, wg_ref[...])
    ba_ref[...] = _dot(h_hi, wbh_ref[...]) + _dot(h_hi, wbl_ref[...]) + _dot(h_lo, wbh_ref[...])


def _odd_in_proj(x, mods, w_in, width):
    b, s, d = x.shape
    wq = w_in[:, :3 * width].astype(BF16)
    wg = w_in[:, 3 * width:4 * width].astype(BF16)
    wba = jnp.pad(w_in[:, 4 * width:], ((0, 0), (0, LANES - 2 * GDN_HEADS)))
    wbh, wbl = _split(wba)
    tm = min(256, s)
    row = lambda c: pl.BlockSpec((None, tm, c), lambda i, j: (i, j, 0))
    full = lambda a: pl.BlockSpec(a.shape, lambda i, j: (0, 0))
    return pl.pallas_call(
        _odd_in_kernel,
        grid=(b, s // tm),
        in_specs=[row(d), pl.BlockSpec((None, 3, d), lambda i, j: (i, 0, 0)),
                  full(wq), full(wg), full(wbh), full(wbl)],
        out_specs=[row(3 * width), row(width), row(LANES)],
        out_shape=[jax.ShapeDtypeStruct((b, s, 3 * width), F32),
                   jax.ShapeDtypeStruct((b, s, width), F32),
                   jax.ShapeDtypeStruct((b, s, LANES), F32)],
        compiler_params=_params("arbitrary", "arbitrary"),
    )(x, mods, wq, wg, wbh, wbl)


def _bmm(a, b):
    return lax.dot_general(a, b, (((2,), (1,)), ((0,), (0,))), preferred_element_type=F32)


def _bmm_nt(a, b):
    return lax.dot_general(a, b, (((2,), (2,)), ((0,), (0,))), preferred_element_type=F32)


def _unit_lower_inverse_minus_eye(a, ii, jj):
    mm = lambda p, q: _bmm(p.astype(BF16), q.astype(BF16))
    diag = (ii // 16) == (jj // 16)
    ad = jnp.where(diag, a, 0.0)
    e = a - ad
    a2 = mm(ad, ad)
    a4 = mm(a2, a2)
    a8 = mm(a4, a4)
    p = -ad
    for sq in (a2, a4, a8):
        p = p + sq + mm(p, sq)
    m = e + mm(p, e)
    m2 = mm(m, m)
    q = m2 - m - mm(m, m2)
    return q + p + mm(q, p)


def _gdn_kernel(qkv_ref, gate_ref, ba_ref, cw_ref, alog_ref, dtb_ref, onorm_ref, ltri_ref, y_ref,
                ext_ref, act_ref, state_ref, u_ref, w_ref, attn_ref, qg_ref, kdt_ref, el_ref, *, ts, width, dh):
    i = pl.program_id(1)
    nh = width // dh

    @pl.when(i == 0)
    def _():
        ext_ref[0:SUBLANES, :] = jnp.zeros((SUBLANES, 3 * width), F32)
        state_ref[...] = jnp.zeros(state_ref.shape, F32)

    @pl.when(i > 0)
    def _():
        ext_ref[0:SUBLANES, :] = ext_ref[ts:ts + SUBLANES, :]

    ext_ref[SUBLANES:SUBLANES + ts, :] = qkv_ref[...]

    for cb in range(3 * nh):
        cols = slice(cb * dh, (cb + 1) * dh)
        acc = None
        for k in range(SHORT_CONV):
            lo = SUBLANES - (SHORT_CONV - 1) + k
            term = cw_ref[k:k + 1, cols] * ext_ref[lo:lo + ts, cols]
            acc = term if acc is None else acc + term
        z = _silu(acc)
        if cb < 2 * nh:
            z = z * lax.rsqrt(jnp.sum(z * z, axis=-1, keepdims=True) + RMS_EPS)
            if cb < nh:
                z = z * (dh ** -0.5)
        act_ref[:, cols] = z

    ii = lax.broadcasted_iota(I32, (CHUNK, CHUNK), 0)
    jj = lax.broadcasted_iota(I32, (CHUNK, CHUNK), 1)
    ltri = ltri_ref[...]

    def prep_body(cp, carry):
        r0 = pl.multiple_of(cp * (GDN_GROUP * CHUNK), GDN_GROUP * CHUNK)
        ba = ba_ref[pl.ds(r0, GDN_GROUP * CHUNK), :]
        beta = _sigmoid(ba)
        xg = ba + dtb_ref[...]
        softplus = jnp.maximum(xg, 0.0) + jnp.log1p(jnp.exp(-jnp.abs(xg)))
        g = -jnp.exp(alog_ref[...]) * softplus
        g1 = g.astype(BF16)
        r1 = g - g1.astype(F32)
        g2 = r1.astype(BF16)
        g3 = (r1 - g2.astype(F32)).astype(BF16)
        gc = _dot(ltri, g1) + _dot(ltri, g2) + _dot(ltri, g3)
        gc_t = gc.T
        egc = jnp.exp(gc)

        pairs = [(c, h) for c in range(GDN_GROUP) for h in range(nh)]
        rows_of = lambda c: slice(c * CHUNK, (c + 1) * CHUNK)
        col = lambda arr, lane: jnp.stack([arr[rows_of(c), lane(h):lane(h) + 1] for c, h in pairs])
        beta_c = col(beta, lambda h: h)
        gc_c = col(gc, lambda h: nh + h)
        egc_c = col(egc, lambda h: nh + h)
        gc_r = jnp.stack([gc_t[nh + h:nh + h + 1, rows_of(c)] for c, h in pairs])
        g_last = jnp.stack([gc[(c + 1) * CHUNK - 1:(c + 1) * CHUNK, nh + h:nh + h + 1] for c, h in pairs])
        act = lambda part: jnp.stack([act_ref[pl.ds(r0 + c * CHUNK, CHUNK),
                                              part * width + h * dh:part * width + (h + 1) * dh] for c, h in pairs])
        q, k, v = act(0), act(1), act(2)

        dec = jnp.exp(jnp.where(ii >= jj, gc_c - gc_r, 0.0))
        kb = k * beta_c
        k_bf = k.astype(BF16)
        a = _bmm_nt(kb.astype(BF16), k_bf) * jnp.where(ii > jj, dec, 0.0)
        attn = _bmm_nt(q.astype(BF16), k_bf) * jnp.where(ii >= jj, dec, 0.0)
        tm1 = _unit_lower_inverse_minus_eye(a, ii, jj).astype(BF16)
        vb = v * beta_c
        kbg = kb * egc_c
        n0 = pl.multiple_of(cp * (GDN_GROUP * nh), GDN_GROUP * nh)
        sl = pl.ds(n0, GDN_GROUP * nh)
        u_ref[sl] = vb + _bmm(tm1, vb.astype(BF16))
        w_ref[sl] = (kbg + _bmm(tm1, kbg.astype(BF16))).astype(BF16)
        attn_ref[sl] = attn.astype(BF16)
        qg_ref[sl] = (q * egc_c).astype(BF16)
        kdt_ref[sl] = jnp.swapaxes(k * jnp.exp(g_last - gc_c), 1, 2).astype(BF16)
        el_ref[sl] = jnp.broadcast_to(jnp.exp(g_last), (GDN_GROUP * nh, 1, dh))
        return carry

    lax.fori_loop(jnp.int32(0), jnp.int32(ts // (GDN_GROUP * CHUNK)), prep_body, 0)

    def chunk_body(c, carry):
        r0 = pl.multiple_of(c * CHUNK, CHUNK)
        sl = pl.ds(pl.multiple_of(c * nh, nh), nh)
        st = state_ref[...]
        st_bf = st.astype(BF16)
        v_new = u_ref[sl] - _bmm(w_ref[sl], st_bf)
        vn_bf = v_new.astype(BF16)
        o = _bmm(qg_ref[sl], st_bf) + _bmm(attn_ref[sl], vn_bf)
        state_ref[...] = st * el_ref[sl] + _bmm(kdt_ref[sl], vn_bf)
        on = o * lax.rsqrt(jnp.mean(o * o, axis=-1, keepdims=True) + RMS_EPS) * onorm_ref[...]
        for h in range(nh):
            hc = slice(h * dh, (h + 1) * dh)
            y_ref[pl.ds(r0, CHUNK), hc] = (on[h] * _silu(gate_ref[pl.ds(r0, CHUNK), hc])).astype(BF16)
        return carry

    lax.fori_loop(jnp.int32(0), jnp.int32(ts // CHUNK), chunk_body, 0)


def _gdn(qkv, gate, ba, conv_w, a_log, dt_bias, out_norm):
    b, s, w3 = qkv.shape
    width = w3 // 3
    dh = width // GDN_HEADS
    ts = min(512, s)
    pad_a = lambda v: jnp.pad(v.reshape(1, GDN_HEADS), ((0, 0), (GDN_HEADS, LANES - 2 * GDN_HEADS)))
    gc_rows = GDN_GROUP * CHUNK
    ri = lax.broadcasted_iota(I32, (gc_rows, gc_rows), 0)
    ci = lax.broadcasted_iota(I32, (gc_rows, gc_rows), 1)
    ltri = ((ri >= ci) & (ri // CHUNK == ci // CHUNK)).astype(BF16)
    n_mat = (ts // CHUNK) * GDN_HEADS
    kern = functools.partial(_gdn_kernel, ts=ts, width=width, dh=dh)
    row = lambda c: pl.BlockSpec((None, ts, c), lambda i, j: (i, j, 0))
    full = lambda shape: pl.BlockSpec(shape, lambda i, j: (0,) * len(shape))
    return pl.pallas_call(
        kern,
        grid=(b, s // ts),
        in_specs=[row(w3), row(width), row(LANES), full(conv_w.shape), full((1, LANES)), full((1, LANES)),
                  full((1, dh)), full((gc_rows, gc_rows))],
        out_specs=row(width),
        out_shape=jax.ShapeDtypeStruct((b, s, width), BF16),
        scratch_shapes=[pltpu.VMEM((ts + SUBLANES, w3), F32),
                        pltpu.VMEM((ts, w3), F32),
                        pltpu.VMEM((GDN_HEADS, dh, dh), F32),
                        pltpu.VMEM((n_mat, CHUNK, dh), F32),
                        pltpu.VMEM((n_mat, CHUNK, dh), BF16),
                        pltpu.VMEM((n_mat, CHUNK, CHUNK), BF16),
                        pltpu.VMEM((n_mat, CHUNK, dh), BF16),
                        pltpu.VMEM((n_mat, dh, CHUNK), BF16),
                        pltpu.VMEM((n_mat, 1, dh), F32)],
        compiler_params=_params("arbitrary", "arbitrary"),
    )(qkv, gate, ba, conv_w, pad_a(a_log), pad_a(dt_bias), out_norm.reshape(1, dh), ltri)


def kernel(x, c, e_mod_w, e_mod_b, e_w_in, e_pool_w, e_pool_scale, e_kv_norm, e_w_uk, e_w_uv, e_w_out, e_ln_g, e_ln_b, o_mod_w, o_mod_b, o_w_in, o_conv_w, o_a_log, o_dt_bias, o_out_norm, o_w_out, o_ln_g, o_ln_b, f_mod_w, f_mod_b, f_w_up, f_conv_w, f_conv_b, f_w_down, f_ln_g, f_ln_b):
    depth = f_mod_w.shape[0]
    alpha = (2 * depth) ** 0.25
    for layer in range(depth):
        i = layer // 2
        if layer % 2 == 0:
            mods = _modulation(c, e_mod_w, e_mod_b, i)
            u, ql, ckv, qi, kw = _even_in_proj(x, mods, e_w_in[i], e_kv_norm[i], e_w_uk[i])
            y_pool = _pool_mixer(u, e_pool_w[i], e_pool_scale[i])
            y_dsa = _dsa_attention_t(ql, qi, kw, ckv, e_w_uv[i])
            x = _out_proj_ln(x, mods, [y_pool, y_dsa], e_w_out[i], e_ln_g[i], e_ln_b[i], alpha)
        else:
            mods = _modulation(c, o_mod_w, o_mod_b, i)
            width = o_w_out.shape[1]
            qkv, gate, ba = _odd_in_proj(x, mods, o_w_in[i], width)
            y = _gdn(qkv, gate, ba, o_conv_w[i], o_a_log[i], o_dt_bias[i], o_out_norm[i])
            x = _out_proj_ln(x, mods, [y], o_w_out[i], o_ln_g[i], o_ln_b[i], alpha)
        mods = _modulation(c, f_mod_w, f_mod_b, layer)
        act = _ffn_up(x, mods, f_w_up[layer], f_conv_w[layer], f_conv_b[layer])
        x = _out_proj_ln(x, mods, [act], f_w_down[layer], f_ln_g[layer], f_ln_b[layer], alpha)
    return x
```

```python
import functools
import math

import jax
import jax.numpy as jnp
from jax import lax
from jax.experimental import pallas as pl
from jax.experimental.pallas import tpu as pltpu

F32 = jnp.float32
BF16 = jnp.bfloat16
I32 = jnp.int32

LANES = 128
SUBLANES = 8
VMEM_LIMIT = 56 * 1024 * 1024

CHUNK = 64
POOL_WINDOWS = (2, 4, 8, 16)
POOL_HALO = 16
DSA_HEADS = 8
DSA_HEAD_DIM = 64
IDX_HEADS = 4
IDX_DIM = 64
TOPK_MAX = 256
GDN_HEADS = 8
GDN_GROUP = 2
SHORT_CONV = 4
FFN_CONV = 3
LN_EPS = 1e-5
RMS_EPS = 1e-6
NEG_BIG = -1e30
INT_MIN = -(2 ** 31)


def _dot(a, b):
    return jnp.dot(a, b, preferred_element_type=F32)


def _dot_nt(a, b):
    return lax.dot_general(a, b, (((1,), (1,)), ((), ())), preferred_element_type=F32)


def _dot_tn(a, b):
    return lax.dot_general(a, b, (((0,), (0,)), ((), ())), preferred_element_type=F32)


def _split(a):
    hi = a.astype(BF16)
    lo = (a - hi.astype(F32)).astype(BF16)
    return hi, lo


def _sigmoid(x):
    return 1.0 / (1.0 + jnp.exp(-x))


def _silu(x):
    return x * _sigmoid(x)


def _layer_norm(z, g, b):
    mu = jnp.mean(z, axis=-1, keepdims=True)
    zc = z - mu
    var = jnp.mean(zc * zc, axis=-1, keepdims=True)
    return zc * lax.rsqrt(var + LN_EPS) * g + b


def _params(*sem):
    return pltpu.CompilerParams(dimension_semantics=sem, vmem_limit_bytes=VMEM_LIMIT)


def _mod_kernel(c_ref, w_ref, b_ref, o_ref):
    c = c_ref[...]
    a_hi, a_lo = _split(_silu(c))
    w_hi, w_lo = _split(w_ref[...])
    o_ref[...] = _dot(a_hi, w_hi) + _dot(a_hi, w_lo) + _dot(a_lo, w_hi) + b_ref[...]


def _modulation(c, mod_w, mod_b, layer):
    b, d = c.shape
    n3 = mod_w.shape[-1]
    tn = 512
    bias = mod_b.reshape(mod_b.shape[0], 1, n3)
    out = pl.pallas_call(
        _mod_kernel,
        grid=(n3 // tn,),
        in_specs=[pl.BlockSpec((b, d), lambda j: (0, 0)),
                  pl.BlockSpec((None, d, tn), lambda j: (layer, 0, j)),
                  pl.BlockSpec((None, 1, tn), lambda j: (layer, 0, j))],
        out_specs=pl.BlockSpec((b, tn), lambda j: (0, j)),
        out_shape=jax.ShapeDtypeStruct((b, n3), F32),
        compiler_params=_params("arbitrary"),
    )(c, mod_w, bias)
    return out.reshape(b, 3, d)


def _modulate(x, mod):
    return x * (1.0 + mod[1:2, :]) + mod[0:1, :]


def _even_in_kernel(x_ref, mod_ref, wm_ref, wih_ref, wil_ref, kvn_ref, wuk_ref,
                    u_ref, ql_ref, ckv_ref, qi_ref, kw_ref, *, pool_w, dsa_w, kv_rank):
    h = _modulate(x_ref[...], mod_ref[...])
    h_hi, h_lo = _split(h)
    main = _dot(h_hi, wm_ref[...])
    u_ref[...] = main[:, :pool_w]
    q = main[:, pool_w:pool_w + dsa_w]
    ql_ref[...] = _dot(q.astype(BF16), wuk_ref[...]).astype(BF16)
    ckv = main[:, pool_w + dsa_w:]
    ms = jnp.mean(ckv * ckv, axis=-1, keepdims=True)
    ckv_ref[...] = (ckv * lax.rsqrt(ms + RMS_EPS) * kvn_ref[...]).astype(BF16)
    idx = _dot(h_hi, wih_ref[...]) + _dot(h_hi, wil_ref[...]) + _dot(h_lo, wih_ref[...])
    nqi = IDX_HEADS * IDX_DIM
    qi_ref[...] = idx[:, :nqi]
    kw_ref[...] = idx[:, nqi:]


def _even_in_proj(x, mods, w_in, kv_norm, w_uk):
    b, s, d = x.shape
    pool_w = len(POOL_WINDOWS) * LANES
    dsa_w = DSA_HEADS * DSA_HEAD_DIM
    kv_rank = w_uk.shape[1]
    n_main = pool_w + dsa_w + kv_rank
    nqi = IDX_HEADS * IDX_DIM
    n_idx = nqi + LANES
    w_main = w_in[:, :n_main].astype(BF16)
    w_idx = jnp.pad(w_in[:, n_main:], ((0, 0), (0, n_main + n_idx - w_in.shape[1])))
    wih, wil = _split(w_idx)
    wuk_t = jnp.swapaxes(w_uk, 1, 2) * (DSA_HEAD_DIM ** -0.5)
    eye = jnp.eye(DSA_HEADS, dtype=F32)
    wuk_bd = (eye[:, None, :, None] * wuk_t[:, :, None, :]).reshape(dsa_w, DSA_HEADS * kv_rank).astype(BF16)
    tm = min(512, s)
    row = lambda c: pl.BlockSpec((None, tm, c), lambda i, j: (i, j, 0))
    full = lambda a: pl.BlockSpec(a.shape, lambda i, j: (0,) * a.ndim)
    kvn = kv_norm.reshape(1, kv_rank)
    kern = functools.partial(_even_in_kernel, pool_w=pool_w, dsa_w=dsa_w, kv_rank=kv_rank)
    return pl.pallas_call(
        kern,
        grid=(b, s // tm),
        in_specs=[row(d), pl.BlockSpec((None, 3, d), lambda i, j: (i, 0, 0)),
                  full(w_main), full(wih), full(wil), full(kvn), full(wuk_bd)],
        out_specs=[row(pool_w), row(DSA_HEADS * kv_rank), row(kv_rank), row(nqi), row(LANES)],
        out_shape=[jax.ShapeDtypeStruct((b, s, pool_w), F32),
                   jax.ShapeDtypeStruct((b, s, DSA_HEADS * kv_rank), BF16),
                   jax.ShapeDtypeStruct((b, s, kv_rank), BF16),
                   jax.ShapeDtypeStruct((b, s, nqi), F32),
                   jax.ShapeDtypeStruct((b, s, LANES), F32)],
        compiler_params=_params("arbitrary", "arbitrary"),
    )(x, mods, w_main, wih, wil, kvn, wuk_bd)


def _pool_kernel(u_ref, halo_ref, w_ref, scale_ref, o_ref, ext_ref, *, ts):
    i = pl.program_id(1)
    ext_ref[0:POOL_HALO, :] = jnp.where(i > 0, halo_ref[...], 0.0)
    ext_ref[POOL_HALO:POOL_HALO + ts, :] = u_ref[...]
    t = i * ts + lax.broadcasted_iota(I32, (ts, 1), 0)
    for g, win in enumerate(POOL_WINDOWS):
        cols = slice(g * LANES, (g + 1) * LANES)
        cur = ext_ref[POOL_HALO:POOL_HALO + ts, cols]
        acc = cur
        for j in range(1, win):
            acc = acc + ext_ref[POOL_HALO - j:POOL_HALO - j + ts, cols]
        cnt = jnp.minimum(t + 1, win).astype(F32)
        pooled = acc / cnt - cur
        y = _dot(pooled.astype(BF16), w_ref[g]) * scale_ref[:, cols]
        o_ref[:, cols] = y.astype(BF16)


def _pool_mixer(u, pool_w, pool_scale):
    b, s, width = u.shape
    ts = min(512, s)
    kern = functools.partial(_pool_kernel, ts=ts)
    hb = ts // POOL_HALO
    return pl.pallas_call(
        kern,
        grid=(b, s // ts),
        in_specs=[pl.BlockSpec((None, ts, width), lambda i, j: (i, j, 0)),
                  pl.BlockSpec((None, POOL_HALO, width), lambda i, j: (i, jnp.maximum(j * hb - 1, 0), 0)),
                  pl.BlockSpec(pool_w.shape, lambda i, j: (0, 0, 0)),
                  pl.BlockSpec((1, width), lambda i, j: (0, 0))],
        out_specs=pl.BlockSpec((None, ts, width), lambda i, j: (i, j, 0)),
        out_shape=jax.ShapeDtypeStruct((b, s, width), BF16),
        scratch_shapes=[pltpu.VMEM((ts + POOL_HALO, width), F32)],
        compiler_params=_params("arbitrary", "arbitrary"),
    )(u, u, pool_w.astype(BF16), pool_scale.reshape(1, width))


def _dsa_kernel(ql_ref, qi_ref, kwq_ref, kw_ref, ckv_ref, tri_ref, wuv_ref, o_ref,
                ki3_ref, keys_ref, qs_ref, m_ref, l_ref, acc_ref, *, t, sk, topk, rank):
    qb = pl.program_id(1)
    s = kw_ref.shape[0]
    lane = lax.broadcasted_iota(I32, (1, LANES), 1)
    low = lane < IDX_DIM

    @pl.when(qb == 0)
    def _():
        k_hi, k_lo = _split(jnp.where(low, kw_ref[...], 0.0))
        hi_f = k_hi.astype(F32)
        ki3_ref[:, 0:LANES] = (hi_f + pltpu.roll(hi_f, IDX_DIM, 1)).astype(BF16)
        ki3_ref[:, LANES:2 * LANES] = k_lo

    parts = []
    for h in range(IDX_HEADS):
        xq = qi_ref[:, (h // 2) * LANES:(h // 2 + 1) * LANES]
        if h % 2:
            xq = pltpu.roll(xq, IDX_DIM, 1)
        q_hi, q_lo = _split(jnp.where(low, xq, 0.0))
        hi_f = q_hi.astype(F32)
        a = (hi_f + pltpu.roll(q_lo.astype(F32), IDX_DIM, 1)).astype(BF16)
        parts.append(jnp.concatenate([a, q_hi], axis=1))
    qi3 = jnp.concatenate(parts, axis=0)

    wscale = IDX_HEADS ** -0.5 * IDX_DIM ** -0.5
    wi = [kwq_ref[:, IDX_DIM + h:IDX_DIM + h + 1] * wscale for h in range(IDX_HEADS)]

    row = lax.broadcasted_iota(I32, (t, 1), 0)
    adm = qb * t + (row // CHUNK + 1) * CHUNK
    nkb = (qb * t + t + sk - 1) // sk
    col0 = lax.broadcasted_iota(I32, (1, sk), 1)

    def score_body(j, carry):
        off = pl.multiple_of(j * sk, sk)
        sc = _dot_nt(qi3, ki3_ref[pl.ds(off, sk), :])
        score = jnp.zeros((t, sk), F32)
        for h in range(IDX_HEADS):
            score = score + wi[h] * jnp.maximum(sc[h * t:(h + 1) * t, :], 0.0)
        score = jnp.where(score == 0.0, 0.0, score)
        bits = lax.bitcast_convert_type(score, I32)
        key = bits ^ (lax.shift_right_arithmetic(bits, jnp.int32(31)) & jnp.int32(0x7FFFFFFF))
        keys_ref[:, pl.ds(off, sk)] = jnp.where(col0 + off < adm, key, jnp.int32(INT_MIN))
        return carry

    i0 = jnp.int32(0)
    lax.fori_loop(i0, nkb, score_body, 0)

    def count(pred, ref):
        ref_b = jnp.broadcast_to(ref, (t, LANES))

        def body(j, acc):
            off = pl.multiple_of(j * sk, sk)
            for c in range(sk // LANES):
                hit = pred(keys_ref[:, pl.ds(pl.multiple_of(off + c * LANES, LANES), LANES)], ref_b)
                acc = acc + jnp.where(hit, 1.0, 0.0)
            return acc
        acc = lax.fori_loop(i0, nkb, body, jnp.zeros((t, LANES), F32))
        return jnp.sum(acc, axis=1, keepdims=True)

    ge = lambda k, r: k >= r
    kf = float(topk)
    zero = jnp.zeros((t, 1), I32)
    res0 = jnp.where(count(ge, zero) >= kf, zero, jnp.int32(INT_MIN))

    def bit_body(i, res):
        cand = res | lax.shift_left(jnp.int32(1), jnp.int32(30) - i)
        return jnp.where(count(ge, cand) >= kf, cand, res)

    kth = lax.fori_loop(i0, jnp.int32(31), bit_body, res0)
    need = kf - count(lambda k, r: k > r, kth)

    for h in range(DSA_HEADS):
        qs_ref[h * t:(h + 1) * t, :] = ql_ref[:, h * rank:(h + 1) * rank]
    m_ref[...] = jnp.full(m_ref.shape, NEG_BIG, F32)
    l_ref[...] = jnp.zeros(l_ref.shape, F32)
    acc_ref[...] = jnp.zeros(acc_ref.shape, F32)

    def attn_body(j, seen):
        off = pl.multiple_of(j * sk, sk)
        key = keys_ref[:, pl.ds(off, sk)]
        eq = key == kth
        rank_eq = seen + _dot(eq.astype(BF16), tri_ref[...])
        sel = ((key > kth) | (eq & (rank_eq <= need))) & (col0 + off < adm)
        kv = ckv_ref[pl.ds(off, sk), :]
        logits = _dot_nt(qs_ref[...], kv).reshape(DSA_HEADS, t, sk)
        logits = jnp.where(sel[None], logits, NEG_BIG).reshape(DSA_HEADS * t, sk)
        m_old = m_ref[...]
        m_new = jnp.maximum(m_old, jnp.max(logits, axis=1, keepdims=True))
        alpha = jnp.exp(m_old - m_new)
        p = jnp.exp(logits - m_new)
        l_ref[...] = alpha * l_ref[...] + jnp.sum(p, axis=1, keepdims=True)
        acc_ref[...] = alpha * acc_ref[...] + _dot(p.astype(BF16), kv)
        m_ref[...] = m_new
        return seen + jnp.sum(eq.astype(F32), axis=1, keepdims=True)

    lax.fori_loop(i0, nkb, attn_body, jnp.zeros((t, 1), F32))

    o = acc_ref[...] / l_ref[...]
    y = jnp.zeros(o_ref.shape, F32)
    for h in range(DSA_HEADS):
        y = y + _dot(o[h * t:(h + 1) * t, :].astype(BF16), wuv_ref[h])
    o_ref[...] = y.astype(BF16)


def _dsa_attention(ql, qi, kw, ckv, w_uv):
    b, s, _ = ql.shape
    rank = ckv.shape[-1]
    t = 2 * CHUNK
    sk = min(512, s)
    topk = min(TOPK_MAX, s // 4)
    width = DSA_HEADS * DSA_HEAD_DIM
    tri = (lax.broadcasted_iota(I32, (sk, sk), 0) <= lax.broadcasted_iota(I32, (sk, sk), 1)).astype(BF16)
    eye = jnp.eye(DSA_HEADS, dtype=F32)
    wuv = (w_uv[:, :, None, :] * eye[:, None, :, None]).reshape(DSA_HEADS, rank, width).astype(BF16)
    kern = functools.partial(_dsa_kernel, t=t, sk=sk, topk=topk, rank=rank)
    blk = lambda c: pl.BlockSpec((None, t, c), lambda i, j: (i, j, 0))
    seq = lambda c: pl.BlockSpec((None, s, c), lambda i, j: (i, 0, 0))
    return pl.pallas_call(
        kern,
        grid=(b, s // t),
        in_specs=[blk(DSA_HEADS * rank), blk(qi.shape[-1]), blk(LANES), seq(LANES), seq(rank),
                  pl.BlockSpec((sk, sk), lambda i, j: (0, 0)),
                  pl.BlockSpec(wuv.shape, lambda i, j: (0, 0, 0))],
        out_specs=blk(width),
        out_shape=jax.ShapeDtypeStruct((b, s, width), BF16),
        scratch_shapes=[pltpu.VMEM((s, 2 * LANES), BF16),
                        pltpu.VMEM((t, s), I32),
                        pltpu.VMEM((DSA_HEADS * t, rank), BF16),
                        pltpu.VMEM((DSA_HEADS * t, 1), F32),
                        pltpu.VMEM((DSA_HEADS * t, 1), F32),
                        pltpu.VMEM((DSA_HEADS * t, rank), F32)],
        compiler_params=_params("arbitrary", "arbitrary"),
    )(ql, qi, kw, kw, ckv, tri, wuv)


LOG2E = 1.4426950408889634
DSA_QUERY_BLOCK = 256
DSA_KEY_BLOCK = 512


def _dsat_kernel(ql_ref, qi_ref, kwq_ref, kw_ref, ckv_ref, tri_ref, wuv_ref, o_ref,
                 ki3_ref, kvt_ref, keys_ref, qlt_ref, m_ref, l_ref, acc_ref, *, t, sk, topk, rank):
    qb = pl.program_id(1)
    lane = lax.broadcasted_iota(I32, (1, LANES), 1)
    low = lane < IDX_DIM
    nacc = 4 * SUBLANES

    @pl.when(qb == 0)
    def _():
        k_hi, k_lo = _split(jnp.where(low, kw_ref[...], 0.0))
        hi_f = k_hi.astype(F32)
        ki3_ref[:, 0:LANES] = (hi_f + pltpu.roll(hi_f, IDX_DIM, 1)).astype(BF16)
        ki3_ref[:, LANES:2 * LANES] = k_lo
        kvt_ref[...] = ckv_ref[...].astype(F32).T.astype(BF16)

    parts = []
    for h in range(IDX_HEADS):
        xq = qi_ref[:, (h // 2) * LANES:(h // 2 + 1) * LANES]
        if h % 2:
            xq = pltpu.roll(xq, IDX_DIM, 1)
        q_hi, q_lo = _split(jnp.where(low, xq, 0.0))
        hi_f = q_hi.astype(F32)
        a = (hi_f + pltpu.roll(q_lo.astype(F32), IDX_DIM, 1)).astype(BF16)
        parts.append(jnp.concatenate([a, q_hi], axis=1))
    qi3 = jnp.concatenate(parts, axis=0)

    kwq_t = kwq_ref[...].T
    wscale = IDX_HEADS ** -0.5 * IDX_DIM ** -0.5
    wi = [kwq_t[IDX_DIM + h:IDX_DIM + h + 1, :] * wscale for h in range(IDX_HEADS)]

    for h in range(DSA_HEADS):
        qh = ql_ref[:, h * rank:(h + 1) * rank].astype(F32).T * LOG2E
        qlt_ref[:, h * t:(h + 1) * t] = qh.astype(BF16)

    qcol = lax.broadcasted_iota(I32, (1, t), 1)
    adm = qb * t + (qcol // CHUNK + 1) * CHUNK
    nkb = (qb * t + t + sk - 1) // sk
    krow = lax.broadcasted_iota(I32, (sk, 1), 0)
    i0 = jnp.int32(0)

    def score_body(j, carry):
        off = pl.multiple_of(j * sk, sk)
        sc = _dot_nt(ki3_ref[pl.ds(off, sk), :], qi3)
        score = jnp.zeros((sk, t), F32)
        for h in range(IDX_HEADS):
            score = score + wi[h] * jnp.maximum(sc[:, h * t:(h + 1) * t], 0.0)
        score = jnp.where(score == 0.0, 0.0, score)
        bits = lax.bitcast_convert_type(score, I32)
        key = bits ^ (lax.shift_right_arithmetic(bits, jnp.int32(31)) & jnp.int32(0x7FFFFFFF))
        keys_ref[pl.ds(off, sk), :] = jnp.where(krow + off < adm, key, jnp.int32(INT_MIN))
        return carry

    lax.fori_loop(i0, nkb, score_body, 0)

    def count(pred, ref):
        def body(j, acc):
            off = pl.multiple_of(j * sk, sk)
            cols = []
            for c in range(t // LANES):
                cs = slice(c * LANES, (c + 1) * LANES)
                hit = pred(keys_ref[pl.ds(off, sk), cs], ref[:, cs]).astype(F32)
                cols.append(jnp.sum(hit.reshape(sk // nacc, nacc, LANES), axis=0))
            return acc + jnp.concatenate(cols, axis=1)
        acc = lax.fori_loop(i0, nkb, body, jnp.zeros((nacc, t), F32))
        return jnp.sum(acc, axis=0, keepdims=True)

    ge = lambda k, r: k >= r
    kf = float(topk)
    zero = jnp.zeros((1, t), I32)
    res0 = jnp.where(count(ge, zero) >= kf, zero, jnp.int32(INT_MIN))

    def bit_body(i, res):
        cand = res | lax.shift_left(jnp.int32(1), jnp.int32(30) - i)
        return jnp.where(count(ge, cand) >= kf, cand, res)

    kth = lax.fori_loop(i0, jnp.int32(31), bit_body, res0)
    need = kf - count(lambda k, r: k > r, kth)

    m_ref[...] = jnp.full(m_ref.shape, NEG_BIG, F32)
    l_ref[...] = jnp.zeros(l_ref.shape, F32)
    acc_ref[...] = jnp.zeros(acc_ref.shape, F32)

    def attn_body(j, seen):
        off = pl.multiple_of(j * sk, sk)
        key = keys_ref[pl.ds(off, sk), :]
        eq = key == kth
        eq_f = eq.astype(F32)
        rank_eq = seen + _dot(tri_ref[...], eq_f.astype(BF16))
        sel = ((key > kth) | (eq & (rank_eq <= need))) & (krow + off < adm)
        bias = jnp.where(sel, jnp.float32(0.0), jnp.float32(NEG_BIG))
        logits = _dot(ckv_ref[pl.ds(off, sk), :], qlt_ref[...])
        kvt = kvt_ref[:, pl.ds(off, sk)]
        for h in range(DSA_HEADS):
            lg = logits[:, h * t:(h + 1) * t] + bias
            m_old = m_ref[h:h + 1, :]
            m_new = jnp.maximum(m_old, jnp.max(lg, axis=0, keepdims=True))
            alpha = jnp.exp2(m_old - m_new)
            p = jnp.exp2(lg - m_new)
            l_ref[h:h + 1, :] = alpha * l_ref[h:h + 1, :] + jnp.sum(p, axis=0, keepdims=True)
            acc_ref[h] = alpha * acc_ref[h] + _dot(kvt, p.astype(BF16))
            m_ref[h:h + 1, :] = m_new
        return seen + jnp.sum(eq_f, axis=0, keepdims=True)

    lax.fori_loop(i0, nkb, attn_body, jnp.zeros((1, t), F32))

    y_t = jnp.zeros((o_ref.shape[1], t), F32)
    for h in range(DSA_HEADS):
        o_t = acc_ref[h] / l_ref[h:h + 1, :]
        y_t = y_t + _dot(wuv_ref[h], o_t.astype(BF16))
    o_ref[...] = y_t.T.astype(BF16)


def _dsa_attention_t(ql, qi, kw, ckv, w_uv):
    b, s, _ = ql.shape
    rank = ckv.shape[-1]
    t = min(DSA_QUERY_BLOCK, s)
    sk = min(DSA_KEY_BLOCK, s)
    topk = min(TOPK_MAX, s // 4)
    width = DSA_HEADS * DSA_HEAD_DIM
    tri = (lax.broadcasted_iota(I32, (sk, sk), 0) >= lax.broadcasted_iota(I32, (sk, sk), 1)).astype(BF16)
    eye = jnp.eye(DSA_HEADS, dtype=F32)
    wuv_t = jnp.swapaxes(w_uv, 1, 2)
    wuv = (eye[:, :, None, None] * wuv_t[:, None, :, :]).reshape(DSA_HEADS, width, rank).astype(BF16)
    kern = functools.partial(_dsat_kernel, t=t, sk=sk, topk=topk, rank=rank)
    blk = lambda c: pl.BlockSpec((None, t, c), lambda i, j: (i, j, 0))
    seq = lambda c: pl.BlockSpec((None, s, c), lambda i, j: (i, 0, 0))
    return pl.pallas_call(
        kern,
        grid=(b, s // t),
        in_specs=[blk(DSA_HEADS * rank), blk(qi.shape[-1]), blk(LANES), seq(LANES), seq(rank),
                  pl.BlockSpec((sk, sk), lambda i, j: (0, 0)),
                  pl.BlockSpec(wuv.shape, lambda i, j: (0, 0, 0))],
        out_specs=blk(width),
        out_shape=jax.ShapeDtypeStruct((b, s, width), BF16),
        scratch_shapes=[pltpu.VMEM((s, 2 * LANES), BF16),
                        pltpu.VMEM((rank, s), BF16),
                        pltpu.VMEM((s, t), I32),
                        pltpu.VMEM((rank, DSA_HEADS * t), BF16),
                        pltpu.VMEM((DSA_HEADS, t), F32),
                        pltpu.VMEM((DSA_HEADS, t), F32),
                        pltpu.VMEM((DSA_HEADS, rank, t), F32)],
        compiler_params=_params("arbitrary", "arbitrary"),
    )(ql, qi, kw, kw, ckv, tri, wuv)


def _resln_kernel(*refs, n_y, widths, alpha):
    x_ref, mod_ref = refs[0], refs[1]
    y_refs = refs[2:2 + n_y]
    w_ref, g_ref, b_ref, o_ref = refs[2 + n_y:]
    acc = None
    off = 0
    for y_ref, wd in zip(y_refs, widths):
        part = _dot(y_ref[...], w_ref[off:off + wd, :])
        acc = part if acc is None else acc + part
        off += wd
    z = alpha * x_ref[...] + mod_ref[2:3, :] * acc
    o_ref[...] = _layer_norm(z, g_ref[...], b_ref[...])


def _out_proj_ln(x, mods, ys, w, ln_g, ln_b, alpha):
    b, s, d = x.shape
    widths = tuple(y.shape[-1] for y in ys)
    tm = min(512, s)
    kern = functools.partial(_resln_kernel, n_y=len(ys), widths=widths, alpha=alpha)
    row = lambda c: pl.BlockSpec((None, tm, c), lambda i, j: (i, j, 0))
    vec = pl.BlockSpec((1, d), lambda i, j: (0, 0))
    return pl.pallas_call(
        kern,
        grid=(b, s // tm),
        in_specs=[row(d), pl.BlockSpec((None, 3, d), lambda i, j: (i, 0, 0))]
                 + [row(wd) for wd in widths]
                 + [pl.BlockSpec(w.shape, lambda i, j: (0, 0)), vec, vec],
        out_specs=row(d),
        out_shape=jax.ShapeDtypeStruct((b, s, d), F32),
        compiler_params=_params("arbitrary", "arbitrary"),
    )(x, mods, *ys, w.astype(BF16), ln_g.reshape(1, d), ln_b.reshape(1, d))


def _ffn_up_kernel(x_ref, halo_ref, mod_ref, wa_ref, wv_ref, cwa_ref, cwv_ref, cba_ref, cbv_ref,
                   o_ref, h_ref, up_ref, *, tm):
    i = pl.program_id(1)
    j = pl.program_id(2)

    @pl.when(j == 0)
    def _():
        mod = mod_ref[...]
        halo = jnp.where(i > 0, _modulate(halo_ref[...], mod), 0.0)
        h_ref[0:SUBLANES, :] = halo.astype(BF16)
        h_ref[SUBLANES:SUBLANES + tm, :] = _modulate(x_ref[...], mod).astype(BF16)

    def conv(w_ref, cw_ref, cb_ref):
        up_ref[...] = _dot(h_ref[...], w_ref[...])
        out = cb_ref[...]
        for k in range(FFN_CONV):
            lo = SUBLANES - (FFN_CONV - 1) + k
            out = out + cw_ref[k:k + 1, :] * up_ref[lo:lo + tm, :]
        return out

    a = conv(wa_ref, cwa_ref, cba_ref)
    v = conv(wv_ref, cwv_ref, cbv_ref)
    o_ref[...] = (_silu(a) * v).astype(BF16)


def _ffn_up(x, mods, w_up, conv_w, conv_b):
    b, s, d = x.shape
    d_ff = w_up.shape[1] // 2
    tm = min(512, s)
    tn = d_ff
    ncol = d_ff // tn
    wb = w_up.astype(BF16)
    cb = conv_b.reshape(1, 2 * d_ff)
    hb = tm // SUBLANES
    kern = functools.partial(_ffn_up_kernel, tm=tm)
    col_a = lambda r: pl.BlockSpec((r, tn), lambda i, j, k: (0, k))
    col_v = lambda r: pl.BlockSpec((r, tn), lambda i, j, k: (0, k + ncol))
    return pl.pallas_call(
        kern,
        grid=(b, s // tm, ncol),
        in_specs=[pl.BlockSpec((None, tm, d), lambda i, j, k: (i, j, 0)),
                  pl.BlockSpec((None, SUBLANES, d), lambda i, j, k: (i, jnp.maximum(j * hb - 1, 0), 0)),
                  pl.BlockSpec((None, 3, d), lambda i, j, k: (i, 0, 0)),
                  col_a(d), col_v(d), col_a(FFN_CONV), col_v(FFN_CONV), col_a(1), col_v(1)],
        out_specs=pl.BlockSpec((None, tm, tn), lambda i, j, k: (i, j, k)),
        out_shape=jax.ShapeDtypeStruct((b, s, d_ff), BF16),
        scratch_shapes=[pltpu.VMEM((tm + SUBLANES, d), BF16),
                        pltpu.VMEM((tm + SUBLANES, tn), F32)],
        compiler_params=_params("arbitrary", "arbitrary", "arbitrary"),
    )(x, x, mods, wb, wb, conv_w, conv_w, cb, cb)


def _odd_in_kernel(x_ref, mod_ref, wq_ref, wg_ref, wbh_ref, wbl_ref, qkv_ref, gate_ref, ba_ref):
    h = _modulate(x_ref[...], mod_ref[...])
    h_hi, h_lo = _split(h)
    qkv_ref[...] = _dot(h_hi, wq_ref[...])
    gate_ref[...] = _dot(h_hi, wg_ref[...])
    ba_ref[...] = _dot(h_hi, wbh_ref[...]) + _dot(h_hi, wbl_ref[...]) + _dot(h_lo, wbh_ref[...])


def _odd_in_proj(x, mods, w_in, width):
    b, s, d = x.shape
    wq = w_in[:, :3 * width].astype(BF16)
    wg = w_in[:, 3 * width:4 * width].astype(BF16)
    wba = jnp.pad(w_in[:, 4 * width:], ((0, 0), (0, LANES - 2 * GDN_HEADS)))
    wbh, wbl = _split(wba)
    tm = min(256, s)
    row = lambda c: pl.BlockSpec((None, tm, c), lambda i, j: (i, j, 0))
    full = lambda a: pl.BlockSpec(a.shape, lambda i, j: (0, 0))
    return pl.pallas_call(
        _odd_in_kernel,
        grid=(b, s // tm),
        in_specs=[row(d), pl.BlockSpec((None, 3, d), lambda i, j: (i, 0, 0)),
                  full(wq), full(wg), full(wbh), full(wbl)],
        out_specs=[row(3 * width), row(width), row(LANES)],
        out_shape=[jax.ShapeDtypeStruct((b, s, 3 * width), F32),
                   jax.ShapeDtypeStruct((b, s, width), F32),
                   jax.ShapeDtypeStruct((b, s, LANES), F32)],
        compiler_params=_params("arbitrary", "arbitrary"),
    )(x, mods, wq, wg, wbh, wbl)


def _bmm(a, b):
    return lax.dot_general(a, b, (((2,), (1,)), ((0,), (0,))), preferred_element_type=F32)


def _bmm_nt(a, b):
    return lax.dot_general(a, b, (((2,), (2,)), ((0,), (0,))), preferred_element_type=F32)


def _unit_lower_inverse_minus_eye(a, ii, jj):
    mm = lambda p, q: _bmm(p.astype(BF16), q.astype(BF16))
    diag = (ii // 16) == (jj // 16)
    ad = jnp.where(diag, a, 0.0)
    e = a - ad
    a2 = mm(ad, ad)
    a4 = mm(a2, a2)
    a8 = mm(a4, a4)
    p = -ad
    for sq in (a2, a4, a8):
        p = p + sq + mm(p, sq)
    m = e + mm(p, e)
    m2 = mm(m, m)
    q = m2 - m - mm(m, m2)
    return q + p + mm(q, p)


def _gdn_kernel(qkv_ref, gate_ref, ba_ref, cw_ref, alog_ref, dtb_ref, onorm_ref, ltri_ref, y_ref,
                ext_ref, act_ref, state_ref, u_ref, w_ref, attn_ref, qg_ref, kdt_ref, el_ref, *, ts, width, dh):
    i = pl.program_id(1)
    nh = width // dh

    @pl.when(i == 0)
    def _():
        ext_ref[0:SUBLANES, :] = jnp.zeros((SUBLANES, 3 * width), F32)
        state_ref[...] = jnp.zeros(state_ref.shape, F32)

    @pl.when(i > 0)
    def _():
        ext_ref[0:SUBLANES, :] = ext_ref[ts:ts + SUBLANES, :]

    ext_ref[SUBLANES:SUBLANES + ts, :] = qkv_ref[...]

    for cb in range(3 * nh):
        cols = slice(cb * dh, (cb + 1) * dh)
        acc = None
        for k in range(SHORT_CONV):
            lo = SUBLANES - (SHORT_CONV - 1) + k
            term = cw_ref[k:k + 1, cols] * ext_ref[lo:lo + ts, cols]
            acc = term if acc is None else acc + term
        z = _silu(acc)
        if cb < 2 * nh:
            z = z * lax.rsqrt(jnp.sum(z * z, axis=-1, keepdims=True) + RMS_EPS)
            if cb < nh:
                z = z * (dh ** -0.5)
        act_ref[:, cols] = z

    ii = lax.broadcasted_iota(I32, (CHUNK, CHUNK), 0)
    jj = lax.broadcasted_iota(I32, (CHUNK, CHUNK), 1)
    ltri = ltri_ref[...]

    def prep_body(cp, carry):
        r0 = pl.multiple_of(cp * (GDN_GROUP * CHUNK), GDN_GROUP * CHUNK)
        ba = ba_ref[pl.ds(r0, GDN_GROUP * CHUNK), :]
        beta = _sigmoid(ba)
        xg = ba + dtb_ref[...]
        softplus = jnp.maximum(xg, 0.0) + jnp.log1p(jnp.exp(-jnp.abs(xg)))
        g = -jnp.exp(alog_ref[...]) * softplus
        g1 = g.astype(BF16)
        r1 = g - g1.astype(F32)
        g2 = r1.astype(BF16)
        g3 = (r1 - g2.astype(F32)).astype(BF16)
        gc = _dot(ltri, g1) + _dot(ltri, g2) + _dot(ltri, g3)
        gc_t = gc.T
        egc = jnp.exp(gc)

        pairs = [(c, h) for c in range(GDN_GROUP) for h in range(nh)]
        rows_of = lambda c: slice(c * CHUNK, (c + 1) * CHUNK)
        col = lambda arr, lane: jnp.stack([arr[rows_of(c), lane(h):lane(h) + 1] for c, h in pairs])
        beta_c = col(beta, lambda h: h)
        gc_c = col(gc, lambda h: nh + h)
        egc_c = col(egc, lambda h: nh + h)
        gc_r = jnp.stack([gc_t[nh + h:nh + h + 1, rows_of(c)] for c, h in pairs])
        g_last = jnp.stack([gc[(c + 1) * CHUNK - 1:(c + 1) * CHUNK, nh + h:nh + h + 1] for c, h in pairs])
        act = lambda part: jnp.stack([act_ref[pl.ds(r0 + c * CHUNK, CHUNK),
                                              part * width + h * dh:part * width + (h + 1) * dh] for c, h in pairs])
        q, k, v = act(0), act(1), act(2)

        dec = jnp.exp(jnp.where(ii >= jj, gc_c - gc_r, 0.0))
        kb = k * beta_c
        k_bf = k.astype(BF16)
        a = _bmm_nt(kb.astype(BF16), k_bf) * jnp.where(ii > jj, dec, 0.0)
        attn = _bmm_nt(q.astype(BF16), k_bf) * jnp.where(ii >= jj, dec, 0.0)
        tm1 = _unit_lower_inverse_minus_eye(a, ii, jj).astype(BF16)
        vb = v * beta_c
        kbg = kb * egc_c
        n0 = pl.multiple_of(cp * (GDN_GROUP * nh), GDN_GROUP * nh)
        sl = pl.ds(n0, GDN_GROUP * nh)
        u_ref[sl] = vb + _bmm(tm1, vb.astype(BF16))
        w_ref[sl] = (kbg + _bmm(tm1, kbg.astype(BF16))).astype(BF16)
        attn_ref[sl] = attn.astype(BF16)
        qg_ref[sl] = (q * egc_c).astype(BF16)
        kdt_ref[sl] = jnp.swapaxes(k * jnp.exp(g_last - gc_c), 1, 2).astype(BF16)
        el_ref[sl] = jnp.broadcast_to(jnp.exp(g_last), (GDN_GROUP * nh, 1, dh))
        return carry

    lax.fori_loop(jnp.int32(0), jnp.int32(ts // (GDN_GROUP * CHUNK)), prep_body, 0)

    def chunk_body(c, carry):
        r0 = pl.multiple_of(c * CHUNK, CHUNK)
        sl = pl.ds(pl.multiple_of(c * nh, nh), nh)
        st = state_ref[...]
        st_bf = st.astype(BF16)
        v_new = u_ref[sl] - _bmm(w_ref[sl], st_bf)
        vn_bf = v_new.astype(BF16)
        o = _bmm(qg_ref[sl], st_bf) + _bmm(attn_ref[sl], vn_bf)
        state_ref[...] = st * el_ref[sl] + _bmm(kdt_ref[sl], vn_bf)
        on = o * lax.rsqrt(jnp.mean(o * o, axis=-1, keepdims=True) + RMS_EPS) * onorm_ref[...]
        for h in range(nh):
            hc = slice(h * dh, (h + 1) * dh)
            y_ref[pl.ds(r0, CHUNK), hc] = (on[h] * _silu(gate_ref[pl.ds(r0, CHUNK), hc])).astype(BF16)
        return carry

    lax.fori_loop(jnp.int32(0), jnp.int32(ts // CHUNK), chunk_body, 0)


def _gdn(qkv, gate, ba, conv_w, a_log, dt_bias, out_norm):
    b, s, w3 = qkv.shape
    width = w3 // 3
    dh = width // GDN_HEADS
    ts = min(512, s)
    pad_a = lambda v: jnp.pad(v.reshape(1, GDN_HEADS), ((0, 0), (GDN_HEADS, LANES - 2 * GDN_HEADS)))
    gc_rows = GDN_GROUP * CHUNK
    ri = lax.broadcasted_iota(I32, (gc_rows, gc_rows), 0)
    ci = lax.broadcasted_iota(I32, (gc_rows, gc_rows), 1)
    ltri = ((ri >= ci) & (ri // CHUNK == ci // CHUNK)).astype(BF16)
    n_mat = (ts // CHUNK) * GDN_HEADS
    kern = functools.partial(_gdn_kernel, ts=ts, width=width, dh=dh)
    row = lambda c: pl.BlockSpec((None, ts, c), lambda i, j: (i, j, 0))
    full = lambda shape: pl.BlockSpec(shape, lambda i, j: (0,) * len(shape))
    return pl.pallas_call(
        kern,
        grid=(b, s // ts),
        in_specs=[row(w3), row(width), row(LANES), full(conv_w.shape), full((1, LANES)), full((1, LANES)),
                  full((1, dh)), full((gc_rows, gc_rows))],
        out_specs=row(width),
        out_shape=jax.ShapeDtypeStruct((b, s, width), BF16),
        scratch_shapes=[pltpu.VMEM((ts + SUBLANES, w3), F32),
                        pltpu.VMEM((ts, w3), F32),
                        pltpu.VMEM((GDN_HEADS, dh, dh), F32),
                        pltpu.VMEM((n_mat, CHUNK, dh), F32),
                        pltpu.VMEM((n_mat, CHUNK, dh), BF16),
                        pltpu.VMEM((n_mat, CHUNK, CHUNK), BF16),
                        pltpu.VMEM((n_mat, CHUNK, dh), BF16),
                        pltpu.VMEM((n_mat, dh, CHUNK), BF16),
                        pltpu.VMEM((n_mat, 1, dh), F32)],
        compiler_params=_params("arbitrary", "arbitrary"),
    )(qkv, gate, ba, conv_w, pad_a(a_log), pad_a(dt_bias), out_norm.reshape(1, dh), ltri)


def kernel(x, c, e_mod_w, e_mod_b, e_w_in, e_pool_w, e_pool_scale, e_kv_norm, e_w_uk, e_w_uv, e_w_out, e_ln_g, e_ln_b, o_mod_w, o_mod_b, o_w_in, o_conv_w, o_a_log, o_dt_bias, o_out_norm, o_w_out, o_ln_g, o_ln_b, f_mod_w, f_mod_b, f_w_up, f_conv_w, f_conv_b, f_w_down, f_ln_g, f_ln_b):
    depth = f_mod_w.shape[0]
    alpha = (2 * depth) ** 0.25
    for layer in range(depth):
        i = layer // 2
        if layer % 2 == 0:
            mods = _modulation(c, e_mod_w, e_mod_b, i)
            u, ql, ckv, qi, kw = _even_in_proj(x, mods, e_w_in[i], e_kv_norm[i], e_w_uk[i])
            y_pool = _pool_mixer(u, e_pool_w[i], e_pool_scale[i])
            y_dsa = _dsa_attention_t(ql, qi, kw, ckv, e_w_uv[i])
            x = _out_proj_ln(x, mods, [y_pool, y_dsa], e_w_out[i], e_ln_g[i], e_ln_b[i], alpha)
        else:
            mods = _modulation(c, o_mod_w, o_mod_b, i)
            width = o_w_out.shape[1]
            qkv, gate, ba = _odd_in_proj(x, mods, o_w_in[i], width)
            y = _gdn(qkv, gate, ba, o_conv_w[i], o_a_log[i], o_dt_bias[i], o_out_norm[i])
            x = _out_proj_ln(x, mods, [y], o_w_out[i], o_ln_g[i], o_ln_b[i], alpha)
        mods = _modulation(c, f_mod_w, f_mod_b, layer)
        act = _ffn_up(x, mods, f_w_up[layer], f_conv_w[layer], f_conv_b[layer])
        x = _out_proj_ln(x, mods, [act], f_w_down[layer], f_ln_g[layer], f_ln_b[layer], alpha)
    return x
```

```python
import functools
import math

import jax
import jax.numpy as jnp
from jax import lax
from jax.experimental import pallas as pl
from jax.experimental.pallas import tpu as pltpu

F32 = jnp.float32
BF16 = jnp.bfloat16
I32 = jnp.int32

LANES = 128
SUBLANES = 8
VMEM_LIMIT = 56 * 1024 * 1024

CHUNK = 64
POOL_WINDOWS = (2, 4, 8, 16)
POOL_HALO = 16
DSA_HEADS = 8
DSA_HEAD_DIM = 64
IDX_HEADS = 4
IDX_DIM = 64
TOPK_MAX = 256
GDN_HEADS = 8
GDN_GROUP = 4
SHORT_CONV = 4
FFN_CONV = 3
LN_EPS = 1e-5
RMS_EPS = 1e-6
NEG_BIG = -1e30
INT_MIN = -(2 ** 31)


def _dot(a, b):
    return jnp.dot(a, b, preferred_element_type=F32)


def _dot_nt(a, b):
    return lax.dot_general(a, b, (((1,), (1,)), ((), ())), preferred_element_type=F32)


def _dot_tn(a, b):
    return lax.dot_general(a, b, (((0,), (0,)), ((), ())), preferred_element_type=F32)


def _split(a):
    hi = a.astype(BF16)
    lo = (a - hi.astype(F32)).astype(BF16)
    return hi, lo


def _sigmoid(x):
    return 1.0 / (1.0 + jnp.exp(-x))


def _silu(x):
    return x * _sigmoid(x)


def _layer_norm(z, g, b):
    mu = jnp.mean(z, axis=-1, keepdims=True)
    zc = z - mu
    var = jnp.mean(zc * zc, axis=-1, keepdims=True)
    return zc * lax.rsqrt(var + LN_EPS) * g + b


def _params(*sem):
    return pltpu.CompilerParams(dimension_semantics=sem, vmem_limit_bytes=VMEM_LIMIT)


def _mod_kernel(c_ref, w_ref, b_ref, o_ref):
    c = c_ref[...]
    a_hi, a_lo = _split(_silu(c))
    w_hi, w_lo = _split(w_ref[...])
    o_ref[...] = _dot(a_hi, w_hi) + _dot(a_hi, w_lo) + _dot(a_lo, w_hi) + b_ref[...]


def _modulation(c, mod_w, mod_b, layer):
    b, d = c.shape
    n3 = mod_w.shape[-1]
    tn = 512
    bias = mod_b.reshape(mod_b.shape[0], 1, n3)
    out = pl.pallas_call(
        _mod_kernel,
        grid=(n3 // tn,),
        in_specs=[pl.BlockSpec((b, d), lambda j: (0, 0)),
                  pl.BlockSpec((None, d, tn), lambda j: (layer, 0, j)),
                  pl.BlockSpec((None, 1, tn), lambda j: (layer, 0, j))],
        out_specs=pl.BlockSpec((b, tn), lambda j: (0, j)),
        out_shape=jax.ShapeDtypeStruct((b, n3), F32),
        compiler_params=_params("arbitrary"),
    )(c, mod_w, bias)
    return out.reshape(b, 3, d)


def _modulate(x, mod):
    return x * (1.0 + mod[1:2, :]) + mod[0:1, :]


def _even_in_kernel(x_ref, mod_ref, wm_ref, wih_ref, wil_ref, kvn_ref, wuk_ref,
                    u_ref, ql_ref, ckv_ref, qi_ref, kw_ref, *, pool_w, dsa_w, kv_rank):
    h = _modulate(x_ref[...], mod_ref[...])
    h_hi, h_lo = _split(h)
    main = _dot(h_hi, wm_ref[...])
    u_ref[...] = main[:, :pool_w]
    q = main[:, pool_w:pool_w + dsa_w]
    ql_ref[...] = _dot(q.astype(BF16), wuk_ref[...]).astype(BF16)
    ckv = main[:, pool_w + dsa_w:]
    ms = jnp.mean(ckv * ckv, axis=-1, keepdims=True)
    ckv_ref[...] = (ckv * lax.rsqrt(ms + RMS_EPS) * kvn_ref[...]).astype(BF16)
    idx = _dot(h_hi, wih_ref[...]) + _dot(h_hi, wil_ref[...]) + _dot(h_lo, wih_ref[...])
    nqi = IDX_HEADS * IDX_DIM
    qi_ref[...] = idx[:, :nqi]
    kw_ref[...] = idx[:, nqi:]


def _even_in_proj(x, mods, w_in, kv_norm, w_uk):
    b, s, d = x.shape
    pool_w = len(POOL_WINDOWS) * LANES
    dsa_w = DSA_HEADS * DSA_HEAD_DIM
    kv_rank = w_uk.shape[1]
    n_main = pool_w + dsa_w + kv_rank
    nqi = IDX_HEADS * IDX_DIM
    n_idx = nqi + LANES
    w_main = w_in[:, :n_main].astype(BF16)
    w_idx = jnp.pad(w_in[:, n_main:], ((0, 0), (0, n_main + n_idx - w_in.shape[1])))
    wih, wil = _split(w_idx)
    wuk_t = jnp.swapaxes(w_uk, 1, 2) * (DSA_HEAD_DIM ** -0.5)
    eye = jnp.eye(DSA_HEADS, dtype=F32)
    wuk_bd = (eye[:, None, :, None] * wuk_t[:, :, None, :]).reshape(dsa_w, DSA_HEADS * kv_rank).astype(BF16)
    tm = min(512, s)
    row = lambda c: pl.BlockSpec((None, tm, c), lambda i, j: (i, j, 0))
    full = lambda a: pl.BlockSpec(a.shape, lambda i, j: (0,) * a.ndim)
    kvn = kv_norm.reshape(1, kv_rank)
    kern = functools.partial(_even_in_kernel, pool_w=pool_w, dsa_w=dsa_w, kv_rank=kv_rank)
    return pl.pallas_call(
        kern,
        grid=(b, s // tm),
        in_specs=[row(d), pl.BlockSpec((None, 3, d), lambda i, j: (i, 0, 0)),
                  full(w_main), full(wih), full(wil), full(kvn), full(wuk_bd)],
        out_specs=[row(pool_w), row(DSA_HEADS * kv_rank), row(kv_rank), row(nqi), row(LANES)],
        out_shape=[jax.ShapeDtypeStruct((b, s, pool_w), F32),
                   jax.ShapeDtypeStruct((b, s, DSA_HEADS * kv_rank), BF16),
                   jax.ShapeDtypeStruct((b, s, kv_rank), BF16),
                   jax.ShapeDtypeStruct((b, s, nqi), F32),
                   jax.ShapeDtypeStruct((b, s, LANES), F32)],
        compiler_params=_params("arbitrary", "arbitrary"),
    )(x, mods, w_main, wih, wil, kvn, wuk_bd)


def _pool_kernel(u_ref, halo_ref, w_ref, scale_ref, o_ref, ext_ref, *, ts):
    i = pl.program_id(1)
    ext_ref[0:POOL_HALO, :] = jnp.where(i > 0, halo_ref[...], 0.0)
    ext_ref[POOL_HALO:POOL_HALO + ts, :] = u_ref[...]
    t = i * ts + lax.broadcasted_iota(I32, (ts, 1), 0)
    for g, win in enumerate(POOL_WINDOWS):
        cols = slice(g * LANES, (g + 1) * LANES)
        cur = ext_ref[POOL_HALO:POOL_HALO + ts, cols]
        acc = cur
        for j in range(1, win):
            acc = acc + ext_ref[POOL_HALO - j:POOL_HALO - j + ts, cols]
        cnt = jnp.minimum(t + 1, win).astype(F32)
        pooled = acc / cnt - cur
        y = _dot(pooled.astype(BF16), w_ref[g]) * scale_ref[:, cols]
        o_ref[:, cols] = y.astype(BF16)


def _pool_mixer(u, pool_w, pool_scale):
    b, s, width = u.shape
    ts = min(512, s)
    kern = functools.partial(_pool_kernel, ts=ts)
    hb = ts // POOL_HALO
    return pl.pallas_call(
        kern,
        grid=(b, s // ts),
        in_specs=[pl.BlockSpec((None, ts, width), lambda i, j: (i, j, 0)),
                  pl.BlockSpec((None, POOL_HALO, width), lambda i, j: (i, jnp.maximum(j * hb - 1, 0), 0)),
                  pl.BlockSpec(pool_w.shape, lambda i, j: (0, 0, 0)),
                  pl.BlockSpec((1, width), lambda i, j: (0, 0))],
        out_specs=pl.BlockSpec((None, ts, width), lambda i, j: (i, j, 0)),
        out_shape=jax.ShapeDtypeStruct((b, s, width), BF16),
        scratch_shapes=[pltpu.VMEM((ts + POOL_HALO, width), F32)],
        compiler_params=_params("arbitrary", "arbitrary"),
    )(u, u, pool_w.astype(BF16), pool_scale.reshape(1, width))


LOG2E = 1.4426950408889634
DSA_QUERY_BLOCK = 256
DSA_KEY_BLOCK = 512
ONES_ROWS = 16


def _dsat_kernel(ql_ref, qi_ref, kwq_ref, kw_ref, ckv_ref, tri_ref, wuv_ref, o_ref,
                 ki3_ref, kvt_ref, keys_ref, qlt_ref, m_ref, acc_ref, *, t, sk, topk, rank):
    qb = pl.program_id(1)
    lane = lax.broadcasted_iota(I32, (1, LANES), 1)
    low = lane < IDX_DIM
    nacc = 4 * SUBLANES

    @pl.when(qb == 0)
    def _():
        k_hi, k_lo = _split(jnp.where(low, kw_ref[...], 0.0))
        hi_f = k_hi.astype(F32)
        ki3_ref[:, 0:LANES] = (hi_f + pltpu.roll(hi_f, IDX_DIM, 1)).astype(BF16)
        ki3_ref[:, LANES:2 * LANES] = k_lo
        kvt_ref[0:rank, :] = ckv_ref[...].astype(F32).T.astype(BF16)
        kvt_ref[rank:rank + ONES_ROWS, :] = jnp.ones((ONES_ROWS, kvt_ref.shape[1]), BF16)

    parts = []
    for h in range(IDX_HEADS):
        xq = qi_ref[:, (h // 2) * LANES:(h // 2 + 1) * LANES]
        if h % 2:
            xq = pltpu.roll(xq, IDX_DIM, 1)
        q_hi, q_lo = _split(jnp.where(low, xq, 0.0))
        hi_f = q_hi.astype(F32)
        a = (hi_f + pltpu.roll(q_lo.astype(F32), IDX_DIM, 1)).astype(BF16)
        parts.append(jnp.concatenate([a, q_hi], axis=1))
    qi3 = jnp.concatenate(parts, axis=0)

    kwq_t = kwq_ref[...].T
    wscale = IDX_HEADS ** -0.5 * IDX_DIM ** -0.5
    wi = [kwq_t[IDX_DIM + h:IDX_DIM + h + 1, :] * wscale for h in range(IDX_HEADS)]

    for h in range(DSA_HEADS):
        qh = ql_ref[:, h * rank:(h + 1) * rank].astype(F32).T * LOG2E
        qlt_ref[:, h * t:(h + 1) * t] = qh.astype(BF16)

    qcol = lax.broadcasted_iota(I32, (1, t), 1)
    adm = qb * t + (qcol // CHUNK + 1) * CHUNK
    nkb = (qb * t + t + sk - 1) // sk
    krow = lax.broadcasted_iota(I32, (sk, 1), 0)
    i0 = jnp.int32(0)

    def score_body(j, carry):
        off = pl.multiple_of(j * sk, sk)
        sc = _dot_nt(ki3_ref[pl.ds(off, sk), :], qi3)
        score = jnp.zeros((sk, t), F32)
        for h in range(IDX_HEADS):
            score = score + wi[h] * jnp.maximum(sc[:, h * t:(h + 1) * t], 0.0)
        score = jnp.where(score == 0.0, 0.0, score)
        bits = lax.bitcast_convert_type(score, I32)
        key = bits ^ (lax.shift_right_arithmetic(bits, jnp.int32(31)) & jnp.int32(0x7FFFFFFF))
        keys_ref[pl.ds(off, sk), :] = jnp.where(krow + off < adm, key, jnp.int32(INT_MIN))
        return carry

    lax.fori_loop(i0, nkb, score_body, 0)

    def count(pred, ref):
        def body(j, acc):
            off = pl.multiple_of(j * sk, sk)
            cols = []
            for c in range(t // LANES):
                cs = slice(c * LANES, (c + 1) * LANES)
                hit = pred(keys_ref[pl.ds(off, sk), cs], ref[:, cs]).astype(F32)
                cols.append(jnp.sum(hit.reshape(sk // nacc, nacc, LANES), axis=0))
            return acc + jnp.concatenate(cols, axis=1)
        acc = lax.fori_loop(i0, nkb, body, jnp.zeros((nacc, t), F32))
        return jnp.sum(acc, axis=0, keepdims=True)

    ge = lambda k, r: k >= r
    kf = float(topk)
    zero = jnp.zeros((1, t), I32)
    res0 = jnp.where(count(ge, zero) >= kf, zero, jnp.int32(INT_MIN))

    def bit_body(i, res):
        cand = res | lax.shift_left(jnp.int32(1), jnp.int32(30) - i)
        return jnp.where(count(ge, cand) >= kf, cand, res)

    kth = lax.fori_loop(i0, jnp.int32(31), bit_body, res0)
    need = kf - count(lambda k, r: k > r, kth)

    m_ref[...] = jnp.full(m_ref.shape, NEG_BIG, F32)
    acc_ref[...] = jnp.zeros(acc_ref.shape, F32)

    def attn_body(j, seen, ties):
        off = pl.multiple_of(j * sk, sk)
        key = keys_ref[pl.ds(off, sk), :]
        if ties:
            eq = key == kth
            rank_eq = seen + _dot(tri_ref[...], eq.astype(F32).astype(BF16))
            sel = ((key > kth) | (eq & (rank_eq <= need))) & (krow + off < adm)
        else:
            rank_eq = jnp.broadcast_to(seen, (sk, t))
            sel = (key >= kth) & (krow + off < adm)
        bias = jnp.where(sel, jnp.float32(0.0), jnp.float32(NEG_BIG))
        logits = _dot(ckv_ref[pl.ds(off, sk), :], qlt_ref[...])
        kvt = kvt_ref[:, pl.ds(off, sk)]
        for h in range(DSA_HEADS):
            lg = logits[:, h * t:(h + 1) * t] + bias
            m_old = m_ref[h:h + 1, :]
            m_new = jnp.maximum(m_old, jnp.max(lg, axis=0, keepdims=True))
            alpha = jnp.exp2(m_old - m_new)
            p = jnp.exp2(lg - m_new)
            acc_ref[h] = alpha * acc_ref[h] + _dot(kvt, p.astype(BF16))
            m_ref[h:h + 1, :] = m_new
        return rank_eq[sk - 1:sk, :]

    seen0 = jnp.zeros((1, t), F32)
    excess = jnp.max(count(lambda k, r: k == r, kth) - need) > 0.0

    @pl.when(excess)
    def _():
        lax.fori_loop(i0, nkb, functools.partial(attn_body, ties=True), seen0)

    @pl.when(jnp.logical_not(excess))
    def _():
        lax.fori_loop(i0, nkb, functools.partial(attn_body, ties=False), seen0)

    y_t = jnp.zeros((o_ref.shape[1], t), F32)
    for h in range(DSA_HEADS):
        o_t = acc_ref[h, 0:rank, :] / acc_ref[h, rank:rank + 1, :]
        y_t = y_t + _dot(wuv_ref[h], o_t.astype(BF16))
    o_ref[...] = y_t.T.astype(BF16)


def _dsa_attention_t(ql, qi, kw, ckv, w_uv):
    b, s, _ = ql.shape
    rank = ckv.shape[-1]
    t = min(DSA_QUERY_BLOCK, s)
    sk = min(DSA_KEY_BLOCK, s)
    topk = min(TOPK_MAX, s // 4)
    width = DSA_HEADS * DSA_HEAD_DIM
    tri = (lax.broadcasted_iota(I32, (sk, sk), 0) >= lax.broadcasted_iota(I32, (sk, sk), 1)).astype(BF16)
    eye = jnp.eye(DSA_HEADS, dtype=F32)
    wuv_t = jnp.swapaxes(w_uv, 1, 2)
    wuv = (eye[:, :, None, None] * wuv_t[:, None, :, :]).reshape(DSA_HEADS, width, rank).astype(BF16)
    kern = functools.partial(_dsat_kernel, t=t, sk=sk, topk=topk, rank=rank)
    blk = lambda c: pl.BlockSpec((None, t, c), lambda i, j: (i, j, 0))
    seq = lambda c: pl.BlockSpec((None, s, c), lambda i, j: (i, 0, 0))
    return pl.pallas_call(
        kern,
        grid=(b, s // t),
        in_specs=[blk(DSA_HEADS * rank), blk(qi.shape[-1]), blk(LANES), seq(LANES), seq(rank),
                  pl.BlockSpec((sk, sk), lambda i, j: (0, 0)),
                  pl.BlockSpec(wuv.shape, lambda i, j: (0, 0, 0))],
        out_specs=blk(width),
        out_shape=jax.ShapeDtypeStruct((b, s, width), BF16),
        scratch_shapes=[pltpu.VMEM((s, 2 * LANES), BF16),
                        pltpu.VMEM((rank + ONES_ROWS, s), BF16),
                        pltpu.VMEM((s, t), I32),
                        pltpu.VMEM((rank, DSA_HEADS * t), BF16),
                        pltpu.VMEM((DSA_HEADS, t), F32),
                        pltpu.VMEM((DSA_HEADS, rank + ONES_ROWS, t), F32)],
        compiler_params=_params("arbitrary", "arbitrary"),
    )(ql, qi, kw, kw, ckv, tri, wuv)


def _resln_kernel(*refs, n_y, widths, alpha):
    x_ref, mod_ref = refs[0], refs[1]
    y_refs = refs[2:2 + n_y]
    w_ref, g_ref, b_ref, o_ref = refs[2 + n_y:]
    acc = None
    off = 0
    for y_ref, wd in zip(y_refs, widths):
        part = _dot(y_ref[...], w_ref[off:off + wd, :])
        acc = part if acc is None else acc + part
        off += wd
    z = alpha * x_ref[...] + mod_ref[2:3, :] * acc
    o_ref[...] = _layer_norm(z, g_ref[...], b_ref[...])


def _out_proj_ln(x, mods, ys, w, ln_g, ln_b, alpha):
    b, s, d = x.shape
    widths = tuple(y.shape[-1] for y in ys)
    tm = min(512, s)
    kern = functools.partial(_resln_kernel, n_y=len(ys), widths=widths, alpha=alpha)
    row = lambda c: pl.BlockSpec((None, tm, c), lambda i, j: (i, j, 0))
    vec = pl.BlockSpec((1, d), lambda i, j: (0, 0))
    return pl.pallas_call(
        kern,
        grid=(b, s // tm),
        in_specs=[row(d), pl.BlockSpec((None, 3, d), lambda i, j: (i, 0, 0))]
                 + [row(wd) for wd in widths]
                 + [pl.BlockSpec(w.shape, lambda i, j: (0, 0)), vec, vec],
        out_specs=row(d),
        out_shape=jax.ShapeDtypeStruct((b, s, d), F32),
        compiler_params=_params("arbitrary", "arbitrary"),
    )(x, mods, *ys, w.astype(BF16), ln_g.reshape(1, d), ln_b.reshape(1, d))


def _ffn_up_kernel(x_ref, halo_ref, mod_ref, wa_ref, wv_ref, cwa_ref, cwv_ref, cba_ref, cbv_ref,
                   o_ref, h_ref, up_ref, *, tm):
    i = pl.program_id(1)
    j = pl.program_id(2)

    @pl.when(j == 0)
    def _():
        mod = mod_ref[...]
        halo = jnp.where(i > 0, _modulate(halo_ref[...], mod), 0.0)
        h_ref[0:SUBLANES, :] = halo.astype(BF16)
        h_ref[SUBLANES:SUBLANES + tm, :] = _modulate(x_ref[...], mod).astype(BF16)

    def conv(w_ref, cw_ref, cb_ref):
        up_ref[...] = _dot(h_ref[...], w_ref[...])
        out = cb_ref[...]
        for k in range(FFN_CONV):
            lo = SUBLANES - (FFN_CONV - 1) + k
            out = out + cw_ref[k:k + 1, :] * up_ref[lo:lo + tm, :]
        return out

    a = conv(wa_ref, cwa_ref, cba_ref)
    v = conv(wv_ref, cwv_ref, cbv_ref)
    o_ref[...] = (_silu(a) * v).astype(BF16)


def _ffn_up(x, mods, w_up, conv_w, conv_b):
    b, s, d = x.shape
    d_ff = w_up.shape[1] // 2
    tm = min(512, s)
    tn = d_ff
    ncol = d_ff // tn
    wb = w_up.astype(BF16)
    cb = conv_b.reshape(1, 2 * d_ff)
    hb = tm // SUBLANES
    kern = functools.partial(_ffn_up_kernel, tm=tm)
    col_a = lambda r: pl.BlockSpec((r, tn), lambda i, j, k: (0, k))
    col_v = lambda r: pl.BlockSpec((r, tn), lambda i, j, k: (0, k + ncol))
    return pl.pallas_call(
        kern,
        grid=(b, s // tm, ncol),
        in_specs=[pl.BlockSpec((None, tm, d), lambda i, j, k: (i, j, 0)),
                  pl.BlockSpec((None, SUBLANES, d), lambda i, j, k: (i, jnp.maximum(j * hb - 1, 0), 0)),
                  pl.BlockSpec((None, 3, d), lambda i, j, k: (i, 0, 0)),
                  col_a(d), col_v(d), col_a(FFN_CONV), col_v(FFN_CONV), col_a(1), col_v(1)],
        out_specs=pl.BlockSpec((None, tm, tn), lambda i, j, k: (i, j, k)),
        out_shape=jax.ShapeDtypeStruct((b, s, d_ff), BF16),
        scratch_shapes=[pltpu.VMEM((tm + SUBLANES, d), BF16),
                        pltpu.VMEM((tm + SUBLANES, tn), F32)],
        compiler_params=_params("arbitrary", "arbitrary", "arbitrary"),
    )(x, x, mods, wb, wb, conv_w, conv_w, cb, cb)


def _odd_in_kernel(x_ref, mod_ref, wq_ref, wg_ref, wbh_ref, wbl_ref, qkv_ref, gate_ref, ba_ref):
    h = _modulate(x_ref[...], mod_ref[...])
    h_hi, h_lo = _split(h)
    qkv_ref[...] = _dot(h_hi, wq_ref[...])
    gate_ref[...] = _dot(h_hi, wg_ref[...])
    ba_ref[...] = _dot(h_hi, wbh_ref[...]) + _dot(h_hi, wbl_ref[...]) + _dot(h_lo, wbh_ref[...])


def _odd_in_proj(x, mods, w_in, width):
    b, s, d = x.shape
    wq = w_in[:, :3 * width].astype(BF16)
    wg = w_in[:, 3 * width:4 * width].astype(BF16)
    wba = jnp.pad(w_in[:, 4 * width:], ((0, 0), (0, LANES - 2 * GDN_HEADS)))
    wbh, wbl = _split(wba)
    tm = min(256, s)
    row = lambda c: pl.BlockSpec((None, tm, c), lambda i, j: (i, j, 0))
    full = lambda a: pl.BlockSpec(a.shape, lambda i, j: (0, 0))
    return pl.pallas_call(
        _odd_in_kernel,
        grid=(b, s // tm),
        in_specs=[row(d), pl.BlockSpec((None, 3, d), lambda i, j: (i, 0, 0)),
                  full(wq), full(wg), full(wbh), full(wbl)],
        out_specs=[row(3 * width), row(width), row(LANES)],
        out_shape=[jax.ShapeDtypeStruct((b, s, 3 * width), F32),
                   jax.ShapeDtypeStruct((b, s, width), F32),
                   jax.ShapeDtypeStruct((b, s, LANES), F32)],
        compiler_params=_params("arbitrary", "arbitrary"),
    )(x, mods, wq, wg, wbh, wbl)


def _bmm(a, b):
    return lax.dot_general(a, b, (((2,), (1,)), ((0,), (0,))), preferred_element_type=F32)


def _bmm_nt(a, b):
    return lax.dot_general(a, b, (((2,), (2,)), ((0,), (0,))), preferred_element_type=F32)


def _unit_lower_inverse_minus_eye(a, ii, jj):
    mm = lambda p, q: _bmm(p.astype(BF16), q.astype(BF16))
    diag = (ii // 16) == (jj // 16)
    ad = jnp.where(diag, a, 0.0)
    e = a - ad
    a2 = mm(ad, ad)
    a4 = mm(a2, a2)
    a8 = mm(a4, a4)
    p = -ad
    for sq in (a2, a4, a8):
        p = p + sq + mm(p, sq)
    m = e + mm(p, e)
    m2 = mm(m, m)
    q = m2 - m - mm(m, m2)
    return q + p + mm(q, p)


def _gdn_kernel(qkv_ref, gate_ref, ba_ref, cw_ref, alog_ref, dtb_ref, onorm_ref, ltri_ref, y_ref,
                ext_ref, act_ref, state_ref, u_ref, w_ref, attn_ref, qg_ref, kdt_ref, el_ref, *, ts, width, dh):
    i = pl.program_id(1)
    nh = width // dh

    @pl.when(i == 0)
    def _():
        ext_ref[0:SUBLANES, :] = jnp.zeros((SUBLANES, 3 * width), F32)
        state_ref[...] = jnp.zeros(state_ref.shape, F32)

    @pl.when(i > 0)
    def _():
        ext_ref[0:SUBLANES, :] = ext_ref[ts:ts + SUBLANES, :]

    ext_ref[SUBLANES:SUBLANES + ts, :] = qkv_ref[...]

    for cb in range(3 * nh):
        cols = slice(cb * dh, (cb + 1) * dh)
        acc = None
        for k in range(SHORT_CONV):
            lo = SUBLANES - (SHORT_CONV - 1) + k
            term = cw_ref[k:k + 1, cols] * ext_ref[lo:lo + ts, cols]
            acc = term if acc is None else acc + term
        z = _silu(acc)
        if cb < 2 * nh:
            z = z * lax.rsqrt(jnp.sum(z * z, axis=-1, keepdims=True) + RMS_EPS)
            if cb < nh:
                z = z * (dh ** -0.5)
        act_ref[:, cols] = z

    ii = lax.broadcasted_iota(I32, (CHUNK, CHUNK), 0)
    jj = lax.broadcasted_iota(I32, (CHUNK, CHUNK), 1)
    ltri = ltri_ref[...]

    def prep_body(cp, carry):
        r0 = pl.multiple_of(cp * (GDN_GROUP * CHUNK), GDN_GROUP * CHUNK)
        ba = ba_ref[pl.ds(r0, GDN_GROUP * CHUNK), :]
        beta = _sigmoid(ba)
        xg = ba + dtb_ref[...]
        softplus = jnp.maximum(xg, 0.0) + jnp.log1p(jnp.exp(-jnp.abs(xg)))
        g = -jnp.exp(alog_ref[...]) * softplus
        g1 = g.astype(BF16)
        r1 = g - g1.astype(F32)
        g2 = r1.astype(BF16)
        g3 = (r1 - g2.astype(F32)).astype(BF16)
        gc = _dot(ltri, g1) + _dot(ltri, g2) + _dot(ltri, g3)
        gc_t = gc.T
        egc = jnp.exp(gc)

        pairs = [(c, h) for c in range(GDN_GROUP) for h in range(nh)]
        rows_of = lambda c: slice(c * CHUNK, (c + 1) * CHUNK)
        col = lambda arr, lane: jnp.stack([arr[rows_of(c), lane(h):lane(h) + 1] for c, h in pairs])
        beta_c = col(beta, lambda h: h)
        gc_c = col(gc, lambda h: nh + h)
        egc_c = col(egc, lambda h: nh + h)
        gc_r = jnp.stack([gc_t[nh + h:nh + h + 1, rows_of(c)] for c, h in pairs])
        g_last = jnp.stack([gc[(c + 1) * CHUNK - 1:(c + 1) * CHUNK, nh + h:nh + h + 1] for c, h in pairs])
        act = lambda part: jnp.stack([act_ref[pl.ds(r0 + c * CHUNK, CHUNK),
                                              part * width + h * dh:part * width + (h + 1) * dh] for c, h in pairs])
        q, k, v = act(0), act(1), act(2)

        dec = jnp.exp(jnp.where(ii >= jj, gc_c - gc_r, 0.0))
        kb = k * beta_c
        k_bf = k.astype(BF16)
        a = _bmm_nt(kb.astype(BF16), k_bf) * jnp.where(ii > jj, dec, 0.0)
        attn = _bmm_nt(q.astype(BF16), k_bf) * jnp.where(ii >= jj, dec, 0.0)
        tm1 = _unit_lower_inverse_minus_eye(a, ii, jj).astype(BF16)
        vb = v * beta_c
        kbg = kb * egc_c
        n0 = pl.multiple_of(cp * (GDN_GROUP * nh), GDN_GROUP * nh)
        sl = pl.ds(n0, GDN_GROUP * nh)
        u_ref[sl] = vb + _bmm(tm1, vb.astype(BF16))
        w_ref[sl] = (kbg + _bmm(tm1, kbg.astype(BF16))).astype(BF16)
        attn_ref[sl] = attn.astype(BF16)
        qg_ref[sl] = (q * egc_c).astype(BF16)
        kdt_ref[sl] = jnp.swapaxes(k * jnp.exp(g_last - gc_c), 1, 2).astype(BF16)
        el_ref[sl] = jnp.broadcast_to(jnp.exp(g_last), (GDN_GROUP * nh, 1, dh))
        return carry

    lax.fori_loop(jnp.int32(0), jnp.int32(ts // (GDN_GROUP * CHUNK)), prep_body, 0)

    def chunk_body(c, carry):
        r0 = pl.multiple_of(c * CHUNK, CHUNK)
        sl = pl.ds(pl.multiple_of(c * nh, nh), nh)
        st = state_ref[...]
        st_bf = st.astype(BF16)
        v_new = u_ref[sl] - _bmm(w_ref[sl], st_bf)
        vn_bf = v_new.astype(BF16)
        o = _bmm(qg_ref[sl], st_bf) + _bmm(attn_ref[sl], vn_bf)
        state_ref[...] = st * el_ref[sl] + _bmm(kdt_ref[sl], vn_bf)
        on = o * lax.rsqrt(jnp.mean(o * o, axis=-1, keepdims=True) + RMS_EPS) * onorm_ref[...]
        for h in range(nh):
            hc = slice(h * dh, (h + 1) * dh)
            y_ref[pl.ds(r0, CHUNK), hc] = (on[h] * _silu(gate_ref[pl.ds(r0, CHUNK), hc])).astype(BF16)
        return carry

    lax.fori_loop(jnp.int32(0), jnp.int32(ts // CHUNK), chunk_body, 0)


def _gdn(qkv, gate, ba, conv_w, a_log, dt_bias, out_norm):
    b, s, w3 = qkv.shape
    width = w3 // 3
    dh = width // GDN_HEADS
    ts = min(512, s)
    pad_a = lambda v: jnp.pad(v.reshape(1, GDN_HEADS), ((0, 0), (GDN_HEADS, LANES - 2 * GDN_HEADS)))
    gc_rows = GDN_GROUP * CHUNK
    ri = lax.broadcasted_iota(I32, (gc_rows, gc_rows), 0)
    ci = lax.broadcasted_iota(I32, (gc_rows, gc_rows), 1)
    ltri = ((ri >= ci) & (ri // CHUNK == ci // CHUNK)).astype(BF16)
    n_mat = (ts // CHUNK) * GDN_HEADS
    kern = functools.partial(_gdn_kernel, ts=ts, width=width, dh=dh)
    row = lambda c: pl.BlockSpec((None, ts, c), lambda i, j: (i, j, 0))
    full = lambda shape: pl.BlockSpec(shape, lambda i, j: (0,) * len(shape))
    return pl.pallas_call(
        kern,
        grid=(b, s // ts),
        in_specs=[row(w3), row(width), row(LANES), full(conv_w.shape), full((1, LANES)), full((1, LANES)),
                  full((1, dh)), full((gc_rows, gc_rows))],
        out_specs=row(width),
        out_shape=jax.ShapeDtypeStruct((b, s, width), BF16),
        scratch_shapes=[pltpu.VMEM((ts + SUBLANES, w3), F32),
                        pltpu.VMEM((ts, w3), F32),
                        pltpu.VMEM((GDN_HEADS, dh, dh), F32),
                        pltpu.VMEM((n_mat, CHUNK, dh), F32),
                        pltpu.VMEM((n_mat, CHUNK, dh), BF16),
                        pltpu.VMEM((n_mat, CHUNK, CHUNK), BF16),
                        pltpu.VMEM((n_mat, CHUNK, dh), BF16),
                        pltpu.VMEM((n_mat, dh, CHUNK), BF16),
                        pltpu.VMEM((n_mat, 1, dh), F32)],
        compiler_params=_params("arbitrary", "arbitrary"),
    )(qkv, gate, ba, conv_w, pad_a(a_log), pad_a(dt_bias), out_norm.reshape(1, dh), ltri)


def kernel(x, c, e_mod_w, e_mod_b, e_w_in, e_pool_w, e_pool_scale, e_kv_norm, e_w_uk, e_w_uv, e_w_out, e_ln_g, e_ln_b, o_mod_w, o_mod_b, o_w_in, o_conv_w, o_a_log, o_dt_bias, o_out_norm, o_w_out, o_ln_g, o_ln_b, f_mod_w, f_mod_b, f_w_up, f_conv_w, f_conv_b, f_w_down, f_ln_g, f_ln_b):
    depth = f_mod_w.shape[0]
    alpha = (2 * depth) ** 0.25
    for layer in range(depth):
        i = layer // 2
        if layer % 2 == 0:
            mods = _modulation(c, e_mod_w, e_mod_b, i)
            u, ql, ckv, qi, kw = _even_in_proj(x, mods, e_w_in[i], e_kv_norm[i], e_w_uk[i])
            y_pool = _pool_mixer(u, e_pool_w[i], e_pool_scale[i])
            y_dsa = _dsa_attention_t(ql, qi, kw, ckv, e_w_uv[i])
            x = _out_proj_ln(x, mods, [y_pool, y_dsa], e_w_out[i], e_ln_g[i], e_ln_b[i], alpha)
        else:
            mods = _modulation(c, o_mod_w, o_mod_b, i)
            width = o_w_out.shape[1]
            qkv, gate, ba = _odd_in_proj(x, mods, o_w_in[i], width)
            y = _gdn(qkv, gate, ba, o_conv_w[i], o_a_log[i], o_dt_bias[i], o_out_norm[i])
            x = _out_proj_ln(x, mods, [y], o_w_out[i], o_ln_g[i], o_ln_b[i], alpha)
        mods = _modulation(c, f_mod_w, f_mod_b, layer)
        act = _ffn_up(x, mods, f_w_up[layer], f_conv_w[layer], f_conv_b[layer])
        x = _out_proj_ln(x, mods, [act], f_w_down[layer], f_ln_g[layer], f_ln_b[layer], alpha)
    return x
```

```python
import functools

import jax
import jax.numpy as jnp
from jax import lax
from jax.experimental import pallas as pl
from jax.experimental.pallas import tpu as pltpu

F32 = jnp.float32
BF16 = jnp.bfloat16
I32 = jnp.int32

LANES = 128
SUBLANES = 8
V7X_VMEM_BYTES = 64 * 1024 * 1024
VMEM_LIMIT = V7X_VMEM_BYTES * 7 // 8

ROW_TILE = 512
MOD_COL_TILE = 512
INV_BLOCK = 16

CHUNK = 64
POOL_WINDOWS = (2, 4, 8, 16)
POOL_HALO = 16
DSA_HEADS = 8
DSA_HEAD_DIM = 64
IDX_HEADS = 4
IDX_DIM = 64
TOPK_MAX = 256
GDN_HEADS = 8
GDN_GROUP = 4
SHORT_CONV = 4
FFN_CONV = 3
LN_EPS = 1e-5
RMS_EPS = 1e-6
NEG_BIG = -1e30
INT_MIN = -(2 ** 31)


def _dot(a, b):
    return jnp.dot(a, b, preferred_element_type=F32)


def _dot_nt(a, b):
    return lax.dot_general(a, b, (((1,), (1,)), ((), ())), preferred_element_type=F32)


def _split(a):
    hi = a.astype(BF16)
    lo = (a - hi.astype(F32)).astype(BF16)
    return hi, lo


def _sigmoid(x):
    return 1.0 / (1.0 + jnp.exp(-x))


def _silu(x):
    return x * _sigmoid(x)


def _layer_norm(z, g, b):
    mu = jnp.mean(z, axis=-1, keepdims=True)
    zc = z - mu
    var = jnp.mean(zc * zc, axis=-1, keepdims=True)
    return zc * lax.rsqrt(var + LN_EPS) * g + b


def _params(*sem):
    return pltpu.CompilerParams(dimension_semantics=sem, vmem_limit_bytes=VMEM_LIMIT)


def _mod_kernel(c_ref, w_ref, b_ref, o_ref):
    c = c_ref[...]
    a_hi, a_lo = _split(_silu(c))
    w_hi, w_lo = _split(w_ref[...])
    o_ref[...] = _dot(a_hi, w_hi) + _dot(a_hi, w_lo) + _dot(a_lo, w_hi) + b_ref[...]


def _modulation(c, mod_w, mod_b, layer):
    b, d = c.shape
    n3 = mod_w.shape[-1]
    tn = MOD_COL_TILE
    bias = mod_b.reshape(mod_b.shape[0], 1, n3)
    out = pl.pallas_call(
        _mod_kernel,
        grid=(n3 // tn,),
        in_specs=[pl.BlockSpec((b, d), lambda j: (0, 0)),
                  pl.BlockSpec((None, d, tn), lambda j: (layer, 0, j)),
                  pl.BlockSpec((None, 1, tn), lambda j: (layer, 0, j))],
        out_specs=pl.BlockSpec((b, tn), lambda j: (0, j)),
        out_shape=jax.ShapeDtypeStruct((b, n3), F32),
        compiler_params=_params("arbitrary"),
    )(c, mod_w, bias)
    return out.reshape(b, 3, d)


def _modulate(x, mod):
    return x * (1.0 + mod[1:2, :]) + mod[0:1, :]


def _even_in_kernel(x_ref, mod_ref, wm_ref, wih_ref, wil_ref, kvn_ref, wuk_ref,
                    u_ref, ql_ref, ckv_ref, qi_ref, kw_ref, *, pool_w, dsa_w, kv_rank):
    h = _modulate(x_ref[...], mod_ref[...])
    h_hi, h_lo = _split(h)
    main = _dot(h_hi, wm_ref[...])
    u_ref[...] = main[:, :pool_w]
    q = main[:, pool_w:pool_w + dsa_w]
    ql_ref[...] = _dot(q.astype(BF16), wuk_ref[...]).astype(BF16)
    ckv = main[:, pool_w + dsa_w:]
    ms = jnp.mean(ckv * ckv, axis=-1, keepdims=True)
    ckv_ref[...] = (ckv * lax.rsqrt(ms + RMS_EPS) * kvn_ref[...]).astype(BF16)
    idx = _dot(h_hi, wih_ref[...]) + _dot(h_hi, wil_ref[...]) + _dot(h_lo, wih_ref[...])
    nqi = IDX_HEADS * IDX_DIM
    qi_ref[...] = idx[:, :nqi]
    kw_ref[...] = idx[:, nqi:]


def _even_in_proj(x, mods, w_in, kv_norm, w_uk):
    b, s, d = x.shape
    pool_w = len(POOL_WINDOWS) * LANES
    dsa_w = DSA_HEADS * DSA_HEAD_DIM
    kv_rank = w_uk.shape[1]
    n_main = pool_w + dsa_w + kv_rank
    nqi = IDX_HEADS * IDX_DIM
    n_idx = nqi + LANES
    w_main = w_in[:, :n_main].astype(BF16)
    w_idx = jnp.pad(w_in[:, n_main:], ((0, 0), (0, n_main + n_idx - w_in.shape[1])))
    wih, wil = _split(w_idx)
    wuk_t = jnp.swapaxes(w_uk, 1, 2) * (DSA_HEAD_DIM ** -0.5)
    eye = jnp.eye(DSA_HEADS, dtype=F32)
    wuk_bd = (eye[:, None, :, None] * wuk_t[:, :, None, :]).reshape(dsa_w, DSA_HEADS * kv_rank).astype(BF16)
    tm = min(ROW_TILE, s)
    row = lambda c: pl.BlockSpec((None, tm, c), lambda i, j: (i, j, 0))
    full = lambda a: pl.BlockSpec(a.shape, lambda i, j: (0,) * a.ndim)
    kvn = kv_norm.reshape(1, kv_rank)
    kern = functools.partial(_even_in_kernel, pool_w=pool_w, dsa_w=dsa_w, kv_rank=kv_rank)
    return pl.pallas_call(
        kern,
        grid=(b, s // tm),
        in_specs=[row(d), pl.BlockSpec((None, 3, d), lambda i, j: (i, 0, 0)),
                  full(w_main), full(wih), full(wil), full(kvn), full(wuk_bd)],
        out_specs=[row(pool_w), row(DSA_HEADS * kv_rank), row(kv_rank), row(nqi), row(LANES)],
        out_shape=[jax.ShapeDtypeStruct((b, s, pool_w), F32),
                   jax.ShapeDtypeStruct((b, s, DSA_HEADS * kv_rank), BF16),
                   jax.ShapeDtypeStruct((b, s, kv_rank), BF16),
                   jax.ShapeDtypeStruct((b, s, nqi), F32),
                   jax.ShapeDtypeStruct((b, s, LANES), F32)],
        compiler_params=_params("arbitrary", "arbitrary"),
    )(x, mods, w_main, wih, wil, kvn, wuk_bd)


def _pool_kernel(u_ref, halo_ref, w_ref, scale_ref, o_ref, ext_ref, *, ts):
    i = pl.program_id(1)
    ext_ref[0:POOL_HALO, :] = jnp.where(i > 0, halo_ref[...], 0.0)
    ext_ref[POOL_HALO:POOL_HALO + ts, :] = u_ref[...]
    t = i * ts + lax.broadcasted_iota(I32, (ts, 1), 0)
    for g, win in enumerate(POOL_WINDOWS):
        cols = slice(g * LANES, (g + 1) * LANES)
        cur = ext_ref[POOL_HALO:POOL_HALO + ts, cols]
        acc = cur
        for j in range(1, win):
            acc = acc + ext_ref[POOL_HALO - j:POOL_HALO - j + ts, cols]
        cnt = jnp.minimum(t + 1, win).astype(F32)
        pooled = acc / cnt - cur
        y = _dot(pooled.astype(BF16), w_ref[g]) * scale_ref[:, cols]
        o_ref[:, cols] = y.astype(BF16)


def _pool_mixer(u, pool_w, pool_scale):
    b, s, width = u.shape
    ts = min(ROW_TILE, s)
    kern = functools.partial(_pool_kernel, ts=ts)
    hb = ts // POOL_HALO
    return pl.pallas_call(
        kern,
        grid=(b, s // ts),
        in_specs=[pl.BlockSpec((None, ts, width), lambda i, j: (i, j, 0)),
                  pl.BlockSpec((None, POOL_HALO, width), lambda i, j: (i, jnp.maximum(j * hb - 1, 0), 0)),
                  pl.BlockSpec(pool_w.shape, lambda i, j: (0, 0, 0)),
                  pl.BlockSpec((1, width), lambda i, j: (0, 0))],
        out_specs=pl.BlockSpec((None, ts, width), lambda i, j: (i, j, 0)),
        out_shape=jax.ShapeDtypeStruct((b, s, width), BF16),
        scratch_shapes=[pltpu.VMEM((ts + POOL_HALO, width), F32)],
        compiler_params=_params("arbitrary", "arbitrary"),
    )(u, u, pool_w.astype(BF16), pool_scale.reshape(1, width))


LOG2E = 1.4426950408889634
DSA_QUERY_BLOCK = 256
DSA_KEY_BLOCK = 512
ONES_ROWS = 16


def _dsat_kernel(ql_ref, qi_ref, kwq_ref, kw_ref, ckv_ref, tri_ref, wuv_ref, o_ref,
                 ki3_ref, kvt_ref, keys_ref, qlt_ref, m_ref, acc_ref, *, t, sk, topk, rank):
    qb = pl.program_id(1)
    lane = lax.broadcasted_iota(I32, (1, LANES), 1)
    low = lane < IDX_DIM
    nacc = 4 * SUBLANES

    @pl.when(qb == 0)
    def _():
        k_hi, k_lo = _split(jnp.where(low, kw_ref[...], 0.0))
        hi_f = k_hi.astype(F32)
        ki3_ref[:, 0:LANES] = (hi_f + pltpu.roll(hi_f, IDX_DIM, 1)).astype(BF16)
        ki3_ref[:, LANES:2 * LANES] = k_lo
        kvt_ref[0:rank, :] = ckv_ref[...].astype(F32).T.astype(BF16)
        kvt_ref[rank:rank + ONES_ROWS, :] = jnp.ones((ONES_ROWS, kvt_ref.shape[1]), BF16)

    parts = []
    for h in range(IDX_HEADS):
        xq = qi_ref[:, (h // 2) * LANES:(h // 2 + 1) * LANES]
        if h % 2:
            xq = pltpu.roll(xq, IDX_DIM, 1)
        q_hi, q_lo = _split(jnp.where(low, xq, 0.0))
        hi_f = q_hi.astype(F32)
        a = (hi_f + pltpu.roll(q_lo.astype(F32), IDX_DIM, 1)).astype(BF16)
        parts.append(jnp.concatenate([a, q_hi], axis=1))
    qi3 = jnp.concatenate(parts, axis=0)

    kwq_t = kwq_ref[...].T
    wscale = IDX_HEADS ** -0.5 * IDX_DIM ** -0.5
    wi = [kwq_t[IDX_DIM + h:IDX_DIM + h + 1, :] * wscale for h in range(IDX_HEADS)]

    for h in range(DSA_HEADS):
        qh = ql_ref[:, h * rank:(h + 1) * rank].astype(F32).T * LOG2E
        qlt_ref[:, h * t:(h + 1) * t] = qh.astype(BF16)

    qcol = lax.broadcasted_iota(I32, (1, t), 1)
    adm = qb * t + (qcol // CHUNK + 1) * CHUNK
    nkb = (qb * t + t + sk - 1) // sk
    krow = lax.broadcasted_iota(I32, (sk, 1), 0)
    i0 = jnp.int32(0)

    def score_body(j, carry):
        off = pl.multiple_of(j * sk, sk)
        sc = _dot_nt(ki3_ref[pl.ds(off, sk), :], qi3)
        score = jnp.zeros((sk, t), F32)
        for h in range(IDX_HEADS):
            score = score + wi[h] * jnp.maximum(sc[:, h * t:(h + 1) * t], 0.0)
        score = jnp.where(score == 0.0, 0.0, score)
        bits = lax.bitcast_convert_type(score, I32)
        key = bits ^ (lax.shift_right_arithmetic(bits, jnp.int32(31)) & jnp.int32(0x7FFFFFFF))
        keys_ref[pl.ds(off, sk), :] = jnp.where(krow + off < adm, key, jnp.int32(INT_MIN))
        return carry

    lax.fori_loop(i0, nkb, score_body, 0)

    def count(pred, ref):
        def body(j, acc):
            off = pl.multiple_of(j * sk, sk)
            cols = []
            for c in range(t // LANES):
                cs = slice(c * LANES, (c + 1) * LANES)
                hit = pred(keys_ref[pl.ds(off, sk), cs], ref[:, cs]).astype(F32)
                cols.append(jnp.sum(hit.reshape(sk // nacc, nacc, LANES), axis=0))
            return acc + jnp.concatenate(cols, axis=1)
        acc = lax.fori_loop(i0, nkb, body, jnp.zeros((nacc, t), F32))
        return jnp.sum(acc, axis=0, keepdims=True)

    ge = lambda k, r: k >= r
    kf = float(topk)
    zero = jnp.zeros((1, t), I32)
    res0 = jnp.where(count(ge, zero) >= kf, zero, jnp.int32(INT_MIN))

    def bit_body(i, res):
        cand = res | lax.shift_left(jnp.int32(1), jnp.int32(30) - i)
        return jnp.where(count(ge, cand) >= kf, cand, res)

    kth = lax.fori_loop(i0, jnp.int32(31), bit_body, res0)
    need = kf - count(lambda k, r: k > r, kth)

    m_ref[...] = jnp.full(m_ref.shape, NEG_BIG, F32)
    acc_ref[...] = jnp.zeros(acc_ref.shape, F32)

    def attn_body(j, seen, ties):
        off = pl.multiple_of(j * sk, sk)
        key = keys_ref[pl.ds(off, sk), :]
        if ties:
            eq = key == kth
            rank_eq = seen + _dot(tri_ref[...], eq.astype(F32).astype(BF16))
            sel = ((key > kth) | (eq & (rank_eq <= need))) & (krow + off < adm)
        else:
            rank_eq = jnp.broadcast_to(seen, (sk, t))
            sel = (key >= kth) & (krow + off < adm)
        bias = jnp.where(sel, jnp.float32(0.0), jnp.float32(NEG_BIG))
        logits = _dot(ckv_ref[pl.ds(off, sk), :], qlt_ref[...])
        kvt = kvt_ref[:, pl.ds(off, sk)]
        for h in range(DSA_HEADS):
            lg = logits[:, h * t:(h + 1) * t] + bias
            m_old = m_ref[h:h + 1, :]
            m_new = jnp.maximum(m_old, jnp.max(lg, axis=0, keepdims=True))
            alpha = jnp.exp2(m_old - m_new)
            p = jnp.exp2(lg - m_new)
            acc_ref[h] = alpha * acc_ref[h] + _dot(kvt, p.astype(BF16))
            m_ref[h:h + 1, :] = m_new
        return rank_eq[sk - 1:sk, :]

    seen0 = jnp.zeros((1, t), F32)
    excess = jnp.max(count(lambda k, r: k == r, kth) - need) > 0.0

    @pl.when(excess)
    def _():
        lax.fori_loop(i0, nkb, functools.partial(attn_body, ties=True), seen0)

    @pl.when(jnp.logical_not(excess))
    def _():
        lax.fori_loop(i0, nkb, functools.partial(attn_body, ties=False), seen0)

    y_t = []
    for h in range(DSA_HEADS):
        o_t = acc_ref[h, 0:rank, :] / acc_ref[h, rank:rank + 1, :]
        y_t.append(_dot(wuv_ref[h], o_t.astype(BF16)))
    o_ref[...] = jnp.concatenate(y_t, axis=0).T.astype(BF16)


def _dsa_attention_t(ql, qi, kw, ckv, w_uv):
    b, s, _ = ql.shape
    rank = ckv.shape[-1]
    t = min(DSA_QUERY_BLOCK, s)
    sk = min(DSA_KEY_BLOCK, s)
    topk = min(TOPK_MAX, s // 4)
    width = DSA_HEADS * DSA_HEAD_DIM
    tri = (lax.broadcasted_iota(I32, (sk, sk), 0) >= lax.broadcasted_iota(I32, (sk, sk), 1)).astype(BF16)
    wuv = jnp.swapaxes(w_uv, 1, 2).astype(BF16)
    kern = functools.partial(_dsat_kernel, t=t, sk=sk, topk=topk, rank=rank)
    blk = lambda c: pl.BlockSpec((None, t, c), lambda i, j: (i, j, 0))
    seq = lambda c: pl.BlockSpec((None, s, c), lambda i, j: (i, 0, 0))
    return pl.pallas_call(
        kern,
        grid=(b, s // t),
        in_specs=[blk(DSA_HEADS * rank), blk(qi.shape[-1]), blk(LANES), seq(LANES), seq(rank),
                  pl.BlockSpec((sk, sk), lambda i, j: (0, 0)),
                  pl.BlockSpec(wuv.shape, lambda i, j: (0, 0, 0))],
        out_specs=blk(width),
        out_shape=jax.ShapeDtypeStruct((b, s, width), BF16),
        scratch_shapes=[pltpu.VMEM((s, 2 * LANES), BF16),
                        pltpu.VMEM((rank + ONES_ROWS, s), BF16),
                        pltpu.VMEM((s, t), I32),
                        pltpu.VMEM((rank, DSA_HEADS * t), BF16),
                        pltpu.VMEM((DSA_HEADS, t), F32),
                        pltpu.VMEM((DSA_HEADS, rank + ONES_ROWS, t), F32)],
        compiler_params=_params("arbitrary", "arbitrary"),
    )(ql, qi, kw, kw, ckv, tri, wuv)


def _resln_kernel(*refs, n_y, widths, alpha):
    x_ref, mod_ref = refs[0], refs[1]
    y_refs = refs[2:2 + n_y]
    w_ref, g_ref, b_ref, o_ref = refs[2 + n_y:]
    acc = None
    off = 0
    for y_ref, wd in zip(y_refs, widths):
        part = _dot(y_ref[...], w_ref[off:off + wd, :])
        acc = part if acc is None else acc + part
        off += wd
    z = alpha * x_ref[...] + mod_ref[2:3, :] * acc
    o_ref[...] = _layer_norm(z, g_ref[...], b_ref[...])


def _out_proj_ln(x, mods, ys, w, ln_g, ln_b, alpha):
    b, s, d = x.shape
    widths = tuple(y.shape[-1] for y in ys)
    tm = min(ROW_TILE, s)
    kern = functools.partial(_resln_kernel, n_y=len(ys), widths=widths, alpha=alpha)
    row = lambda c: pl.BlockSpec((None, tm, c), lambda i, j: (i, j, 0))
    vec = pl.BlockSpec((1, d), lambda i, j: (0, 0))
    return pl.pallas_call(
        kern,
        grid=(b, s // tm),
        in_specs=[row(d), pl.BlockSpec((None, 3, d), lambda i, j: (i, 0, 0))]
                 + [row(wd) for wd in widths]
                 + [pl.BlockSpec(w.shape, lambda i, j: (0, 0)), vec, vec],
        out_specs=row(d),
        out_shape=jax.ShapeDtypeStruct((b, s, d), F32),
        compiler_params=_params("arbitrary", "arbitrary"),
    )(x, mods, *ys, w.astype(BF16), ln_g.reshape(1, d), ln_b.reshape(1, d))


def _ffn_up_kernel(x_ref, halo_ref, mod_ref, wa_ref, wv_ref, cwa_ref, cwv_ref, cba_ref, cbv_ref,
                   o_ref, h_ref, up_ref, *, tm):
    i = pl.program_id(1)
    j = pl.program_id(2)

    @pl.when(j == 0)
    def _():
        mod = mod_ref[...]
        halo = jnp.where(i > 0, _modulate(halo_ref[...], mod), 0.0)
        h_ref[0:SUBLANES, :] = halo.astype(BF16)
        h_ref[SUBLANES:SUBLANES + tm, :] = _modulate(x_ref[...], mod).astype(BF16)

    def conv(w_ref, cw_ref, cb_ref):
        up_ref[...] = _dot(h_ref[...], w_ref[...])
        out = cb_ref[...]
        for k in range(FFN_CONV):
            lo = SUBLANES - (FFN_CONV - 1) + k
            out = out + cw_ref[k:k + 1, :] * up_ref[lo:lo + tm, :]
        return out

    a = conv(wa_ref, cwa_ref, cba_ref)
    v = conv(wv_ref, cwv_ref, cbv_ref)
    o_ref[...] = (_silu(a) * v).astype(BF16)


def _ffn_up(x, mods, w_up, conv_w, conv_b):
    b, s, d = x.shape
    d_ff = w_up.shape[1] // 2
    tm = min(ROW_TILE, s)
    tn = d_ff
    ncol = d_ff // tn
    wb = w_up.astype(BF16)
    cb = conv_b.reshape(1, 2 * d_ff)
    hb = tm // SUBLANES
    kern = functools.partial(_ffn_up_kernel, tm=tm)
    col_a = lambda r: pl.BlockSpec((r, tn), lambda i, j, k: (0, k))
    col_v = lambda r: pl.BlockSpec((r, tn), lambda i, j, k: (0, k + ncol))
    return pl.pallas_call(
        kern,
        grid=(b, s // tm, ncol),
        in_specs=[pl.BlockSpec((None, tm, d), lambda i, j, k: (i, j, 0)),
                  pl.BlockSpec((None, SUBLANES, d), lambda i, j, k: (i, jnp.maximum(j * hb - 1, 0), 0)),
                  pl.BlockSpec((None, 3, d), lambda i, j, k: (i, 0, 0)),
                  col_a(d), col_v(d), col_a(FFN_CONV), col_v(FFN_CONV), col_a(1), col_v(1)],
        out_specs=pl.BlockSpec((None, tm, tn), lambda i, j, k: (i, j, k)),
        out_shape=jax.ShapeDtypeStruct((b, s, d_ff), BF16),
        scratch_shapes=[pltpu.VMEM((tm + SUBLANES, d), BF16),
                        pltpu.VMEM((tm + SUBLANES, tn), F32)],
        compiler_params=_params("arbitrary", "arbitrary", "arbitrary"),
    )(x, x, mods, wb, wb, conv_w, conv_w, cb, cb)


def _odd_in_kernel(x_ref, mod_ref, wq_ref, wg_ref, wbh_ref, wbl_ref, qkv_ref, gate_ref, ba_ref):
    h = _modulate(x_ref[...], mod_ref[...])
    h_hi, h_lo = _split(h)
    qkv_ref[...] = _dot(h_hi, wq_ref[...])
    gate_ref[...] = _dot(h_hi, wg_ref[...])
    ba_ref[...] = _dot(h_hi, wbh_ref[...]) + _dot(h_hi, wbl_ref[...]) + _dot(h_lo, wbh_ref[...])


def _odd_in_proj(x, mods, w_in, width):
    b, s, d = x.shape
    wq = w_in[:, :3 * width].astype(BF16)
    wg = w_in[:, 3 * width:4 * width].astype(BF16)
    wba = jnp.pad(w_in[:, 4 * width:], ((0, 0), (0, LANES - 2 * GDN_HEADS)))
    wbh, wbl = _split(wba)
    tm = min(ROW_TILE, s)
    row = lambda c: pl.BlockSpec((None, tm, c), lambda i, j: (i, j, 0))
    full = lambda a: pl.BlockSpec(a.shape, lambda i, j: (0, 0))
    return pl.pallas_call(
        _odd_in_kernel,
        grid=(b, s // tm),
        in_specs=[row(d), pl.BlockSpec((None, 3, d), lambda i, j: (i, 0, 0)),
                  full(wq), full(wg), full(wbh), full(wbl)],
        out_specs=[row(3 * width), row(width), row(LANES)],
        out_shape=[jax.ShapeDtypeStruct((b, s, 3 * width), F32),
                   jax.ShapeDtypeStruct((b, s, width), F32),
                   jax.ShapeDtypeStruct((b, s, LANES), F32)],
        compiler_params=_params("arbitrary", "arbitrary"),
    )(x, mods, wq, wg, wbh, wbl)


def _bmm(a, b):
    return lax.dot_general(a, b, (((2,), (1,)), ((0,), (0,))), preferred_element_type=F32)


def _bmm_nt(a, b):
    return lax.dot_general(a, b, (((2,), (2,)), ((0,), (0,))), preferred_element_type=F32)


def _unit_lower_inverse_minus_eye(a, ii, jj):
    mm = lambda p, q: _bmm(p.astype(BF16), q.astype(BF16))
    diag = (ii // INV_BLOCK) == (jj // INV_BLOCK)
    ad = jnp.where(diag, a, 0.0)
    e = a - ad
    a2 = mm(ad, ad)
    a4 = mm(a2, a2)
    a8 = mm(a4, a4)
    p = -ad
    for sq in (a2, a4, a8):
        p = p + sq + mm(p, sq)
    m = e + mm(p, e)
    m2 = mm(m, m)
    q = m2 - m - mm(m, m2)
    return q + p + mm(q, p)


def _gdn_kernel(qkv_ref, gate_ref, ba_ref, cw_ref, alog_ref, dtb_ref, onorm_ref, ltri_ref, y_ref,
                ext_ref, act_ref, state_ref, u_ref, w_ref, attn_ref, qg_ref, kdt_ref, el_ref, *, ts, width, dh):
    i = pl.program_id(1)
    nh = width // dh

    @pl.when(i == 0)
    def _():
        ext_ref[0:SUBLANES, :] = jnp.zeros((SUBLANES, 3 * width), F32)
        state_ref[...] = jnp.zeros(state_ref.shape, F32)

    @pl.when(i > 0)
    def _():
        ext_ref[0:SUBLANES, :] = ext_ref[ts:ts + SUBLANES, :]

    ext_ref[SUBLANES:SUBLANES + ts, :] = qkv_ref[...]

    for cb in range(3 * nh):
        cols = slice(cb * dh, (cb + 1) * dh)
        acc = None
        for k in range(SHORT_CONV):
            lo = SUBLANES - (SHORT_CONV - 1) + k
            term = cw_ref[k:k + 1, cols] * ext_ref[lo:lo + ts, cols]
            acc = term if acc is None else acc + term
        z = _silu(acc)
        if cb < 2 * nh:
            z = z * lax.rsqrt(jnp.sum(z * z, axis=-1, keepdims=True) + RMS_EPS)
            if cb < nh:
                z = z * (dh ** -0.5)
        act_ref[:, cols] = z

    ii = lax.broadcasted_iota(I32, (CHUNK, CHUNK), 0)
    jj = lax.broadcasted_iota(I32, (CHUNK, CHUNK), 1)
    ltri = ltri_ref[...]

    def prep_body(cp, carry):
        r0 = pl.multiple_of(cp * (GDN_GROUP * CHUNK), GDN_GROUP * CHUNK)
        ba = ba_ref[pl.ds(r0, GDN_GROUP * CHUNK), :]
        beta = _sigmoid(ba)
        xg = ba + dtb_ref[...]
        softplus = jnp.maximum(xg, 0.0) + jnp.log1p(jnp.exp(-jnp.abs(xg)))
        g = -jnp.exp(alog_ref[...]) * softplus
        g1 = g.astype(BF16)
        r1 = g - g1.astype(F32)
        g2 = r1.astype(BF16)
        g3 = (r1 - g2.astype(F32)).astype(BF16)
        gc = _dot(ltri, g1) + _dot(ltri, g2) + _dot(ltri, g3)
        gc_t = gc.T
        egc = jnp.exp(gc)

        pairs = [(c, h) for c in range(GDN_GROUP) for h in range(nh)]
        rows_of = lambda c: slice(c * CHUNK, (c + 1) * CHUNK)
        col = lambda arr, lane: jnp.stack([arr[rows_of(c), lane(h):lane(h) + 1] for c, h in pairs])
        beta_c = col(beta, lambda h: h)
        gc_c = col(gc, lambda h: nh + h)
        egc_c = col(egc, lambda h: nh + h)
        gc_r = jnp.stack([gc_t[nh + h:nh + h + 1, rows_of(c)] for c, h in pairs])
        g_last = jnp.stack([gc[(c + 1) * CHUNK - 1:(c + 1) * CHUNK, nh + h:nh + h + 1] for c, h in pairs])
        act = lambda part: jnp.stack([act_ref[pl.ds(r0 + c * CHUNK, CHUNK),
                                              part * width + h * dh:part * width + (h + 1) * dh] for c, h in pairs])
        q, k, v = act(0), act(1), act(2)

        dec = jnp.exp(jnp.where(ii >= jj, gc_c - gc_r, 0.0))
        kb = k * beta_c
        k_bf = k.astype(BF16)
        a = _bmm_nt(kb.astype(BF16), k_bf) * jnp.where(ii > jj, dec, 0.0)
        attn = _bmm_nt(q.astype(BF16), k_bf) * jnp.where(ii >= jj, dec, 0.0)
        tm1 = _unit_lower_inverse_minus_eye(a, ii, jj).astype(BF16)
        vb = v * beta_c
        kbg = kb * egc_c
        n0 = pl.multiple_of(cp * (GDN_GROUP * nh), GDN_GROUP * nh)
        sl = pl.ds(n0, GDN_GROUP * nh)
        u_ref[sl] = vb + _bmm(tm1, vb.astype(BF16))
        w_ref[sl] = (kbg + _bmm(tm1, kbg.astype(BF16))).astype(BF16)
        attn_ref[sl] = attn.astype(BF16)
        qg_ref[sl] = (q * egc_c).astype(BF16)
        kdt_ref[sl] = jnp.swapaxes(k * jnp.exp(g_last - gc_c), 1, 2).astype(BF16)
        el_ref[sl] = jnp.broadcast_to(jnp.exp(g_last), (GDN_GROUP * nh, 1, dh))
        return carry

    lax.fori_loop(jnp.int32(0), jnp.int32(ts // (GDN_GROUP * CHUNK)), prep_body, 0)

    def chunk_body(c, carry):
        r0 = pl.multiple_of(c * CHUNK, CHUNK)
        sl = pl.ds(pl.multiple_of(c * nh, nh), nh)
        st = state_ref[...]
        st_bf = st.astype(BF16)
        v_new = u_ref[sl] - _bmm(w_ref[sl], st_bf)
        vn_bf = v_new.astype(BF16)
        o = _bmm(qg_ref[sl], st_bf) + _bmm(attn_ref[sl], vn_bf)
        state_ref[...] = st * el_ref[sl] + _bmm(kdt_ref[sl], vn_bf)
        on = o * lax.rsqrt(jnp.mean(o * o, axis=-1, keepdims=True) + RMS_EPS) * onorm_ref[...]
        for h in range(nh):
            hc = slice(h * dh, (h + 1) * dh)
            y_ref[pl.ds(r0, CHUNK), hc] = (on[h] * _silu(gate_ref[pl.ds(r0, CHUNK), hc])).astype(BF16)
        return carry

    lax.fori_loop(jnp.int32(0), jnp.int32(ts // CHUNK), chunk_body, 0)


def _gdn(qkv, gate, ba, conv_w, a_log, dt_bias, out_norm):
    b, s, w3 = qkv.shape
    width = w3 // 3
    dh = width // GDN_HEADS
    ts = min(ROW_TILE, s)
    pad_a = lambda v: jnp.pad(v.reshape(1, GDN_HEADS), ((0, 0), (GDN_HEADS, LANES - 2 * GDN_HEADS)))
    gc_rows = GDN_GROUP * CHUNK
    ri = lax.broadcasted_iota(I32, (gc_rows, gc_rows), 0)
    ci = lax.broadcasted_iota(I32, (gc_rows, gc_rows), 1)
    ltri = ((ri >= ci) & (ri // CHUNK == ci // CHUNK)).astype(BF16)
    n_mat = (ts // CHUNK) * GDN_HEADS
    kern = functools.partial(_gdn_kernel, ts=ts, width=width, dh=dh)
    row = lambda c: pl.BlockSpec((None, ts, c), lambda i, j: (i, j, 0))
    full = lambda shape: pl.BlockSpec(shape, lambda i, j: (0,) * len(shape))
    return pl.pallas_call(
        kern,
        grid=(b, s // ts),
        in_specs=[row(w3), row(width), row(LANES), full(conv_w.shape), full((1, LANES)), full((1, LANES)),
                  full((1, dh)), full((gc_rows, gc_rows))],
        out_specs=row(width),
        out_shape=jax.ShapeDtypeStruct((b, s, width), BF16),
        scratch_shapes=[pltpu.VMEM((ts + SUBLANES, w3), F32),
                        pltpu.VMEM((ts, w3), F32),
                        pltpu.VMEM((GDN_HEADS, dh, dh), F32),
                        pltpu.VMEM((n_mat, CHUNK, dh), F32),
                        pltpu.VMEM((n_mat, CHUNK, dh), BF16),
                        pltpu.VMEM((n_mat, CHUNK, CHUNK), BF16),
                        pltpu.VMEM((n_mat, CHUNK, dh), BF16),
                        pltpu.VMEM((n_mat, dh, CHUNK), BF16),
                        pltpu.VMEM((n_mat, 1, dh), F32)],
        compiler_params=_params("arbitrary", "arbitrary"),
    )(qkv, gate, ba, conv_w, pad_a(a_log), pad_a(dt_bias), out_norm.reshape(1, dh), ltri)


def kernel(x, c, e_mod_w, e_mod_b, e_w_in, e_pool_w, e_pool_scale, e_kv_norm, e_w_uk, e_w_uv, e_w_out, e_ln_g, e_ln_b, o_mod_w, o_mod_b, o_w_in, o_conv_w, o_a_log, o_dt_bias, o_out_norm, o_w_out, o_ln_g, o_ln_b, f_mod_w, f_mod_b, f_w_up, f_conv_w, f_conv_b, f_w_down, f_ln_g, f_ln_b):
    depth = f_mod_w.shape[0]
    alpha = (2 * depth) ** 0.25
    for layer in range(depth):
        i = layer // 2
        if layer % 2 == 0:
            mods = _modulation(c, e_mod_w, e_mod_b, i)
            u, ql, ckv, qi, kw = _even_in_proj(x, mods, e_w_in[i], e_kv_norm[i], e_w_uk[i])
            y_pool = _pool_mixer(u, e_pool_w[i], e_pool_scale[i])
            y_dsa = _dsa_attention_t(ql, qi, kw, ckv, e_w_uv[i])
            x = _out_proj_ln(x, mods, [y_pool, y_dsa], e_w_out[i], e_ln_g[i], e_ln_b[i], alpha)
        else:
            mods = _modulation(c, o_mod_w, o_mod_b, i)
            width = o_w_out.shape[1]
            qkv, gate, ba = _odd_in_proj(x, mods, o_w_in[i], width)
            y = _gdn(qkv, gate, ba, o_conv_w[i], o_a_log[i], o_dt_bias[i], o_out_norm[i])
            x = _out_proj_ln(x, mods, [y], o_w_out[i], o_ln_g[i], o_ln_b[i], alpha)
        mods = _modulation(c, f_mod_w, f_mod_b, layer)
        act = _ffn_up(x, mods, f_w_up[layer], f_conv_w[layer], f_conv_b[layer])
        x = _out_proj_ln(x, mods, [act], f_w_down[layer], f_ln_g[layer], f_ln_b[layer], alpha)
    return x
```

```python
import functools

import jax
import jax.numpy as jnp
from jax import lax
from jax.experimental import pallas as pl
from jax.experimental.pallas import tpu as pltpu

F32 = jnp.float32
BF16 = jnp.bfloat16
I32 = jnp.int32

LANES = 128
SUBLANES = 8
V7X_VMEM_BYTES = 64 * 1024 * 1024
VMEM_LIMIT = V7X_VMEM_BYTES * 7 // 8

ROW_TILE = 512
MOD_COL_TILE = 512
INV_BLOCK = 16

CHUNK = 64
POOL_WINDOWS = (2, 4, 8, 16)
POOL_HALO = 16
DSA_HEADS = 8
DSA_HEAD_DIM = 64
IDX_HEADS = 4
IDX_DIM = 64
TOPK_MAX = 256
GDN_HEADS = 8
GDN_GROUP = 4
SHORT_CONV = 4
FFN_CONV = 3
LN_EPS = 1e-5
RMS_EPS = 1e-6
NEG_BIG = -1e30
INT_MIN = -(2 ** 31)


def _dot(a, b):
    return jnp.dot(a, b, preferred_element_type=F32)


def _dot_nt(a, b):
    return lax.dot_general(a, b, (((1,), (1,)), ((), ())), preferred_element_type=F32)


def _split(a):
    hi = a.astype(BF16)
    lo = (a - hi.astype(F32)).astype(BF16)
    return hi, lo


def _sigmoid(x):
    return 1.0 / (1.0 + jnp.exp(-x))


def _silu(x):
    return x * _sigmoid(x)


def _layer_norm(z, g, b):
    mu = jnp.mean(z, axis=-1, keepdims=True)
    zc = z - mu
    var = jnp.mean(zc * zc, axis=-1, keepdims=True)
    return zc * lax.rsqrt(var + LN_EPS) * g + b


def _params(*sem):
    return pltpu.CompilerParams(dimension_semantics=sem, vmem_limit_bytes=VMEM_LIMIT)


def _mod_kernel(c_ref, w_ref, b_ref, o_ref):
    c = c_ref[...]
    a_hi, a_lo = _split(_silu(c))
    w_hi, w_lo = _split(w_ref[...])
    o_ref[...] = _dot(a_hi, w_hi) + _dot(a_hi, w_lo) + _dot(a_lo, w_hi) + b_ref[...]


def _modulation(c, mod_w, mod_b, layer):
    b, d = c.shape
    n3 = mod_w.shape[-1]
    tn = MOD_COL_TILE
    bias = mod_b.reshape(mod_b.shape[0], 1, n3)
    out = pl.pallas_call(
        _mod_kernel,
        grid=(n3 // tn,),
        in_specs=[pl.BlockSpec((b, d), lambda j: (0, 0)),
                  pl.BlockSpec((None, d, tn), lambda j: (layer, 0, j)),
                  pl.BlockSpec((None, 1, tn), lambda j: (layer, 0, j))],
        out_specs=pl.BlockSpec((b, tn), lambda j: (0, j)),
        out_shape=jax.ShapeDtypeStruct((b, n3), F32),
        compiler_params=_params("arbitrary"),
    )(c, mod_w, bias)
    return out.reshape(b, 3, d)


def _modulate(x, mod):
    return x * (1.0 + mod[1:2, :]) + mod[0:1, :]


def _even_in_kernel(x_ref, mod_ref, wm_ref, wih_ref, wil_ref, kvn_ref, wuk_ref,
                    u_ref, ql_ref, ckv_ref, qi_ref, kw_ref, *, pool_w, dsa_w, kv_rank):
    h = _modulate(x_ref[...], mod_ref[...])
    h_hi, h_lo = _split(h)
    main = _dot(h_hi, wm_ref[...])
    u_ref[...] = main[:, :pool_w]
    q = main[:, pool_w:pool_w + dsa_w]
    ql_ref[...] = _dot(q.astype(BF16), wuk_ref[...]).astype(BF16)
    ckv = main[:, pool_w + dsa_w:]
    ms = jnp.mean(ckv * ckv, axis=-1, keepdims=True)
    ckv_ref[...] = (ckv * lax.rsqrt(ms + RMS_EPS) * kvn_ref[...]).astype(BF16)
    idx = _dot(h_hi, wih_ref[...]) + _dot(h_hi, wil_ref[...]) + _dot(h_lo, wih_ref[...])
    nqi = IDX_HEADS * IDX_DIM
    qi_ref[...] = idx[:, :nqi]
    kw_ref[...] = idx[:, nqi:]


def _even_in_proj(x, mods, w_in, kv_norm, w_uk):
    b, s, d = x.shape
    pool_w = len(POOL_WINDOWS) * LANES
    dsa_w = DSA_HEADS * DSA_HEAD_DIM
    kv_rank = w_uk.shape[1]
    n_main = pool_w + dsa_w + kv_rank
    nqi = IDX_HEADS * IDX_DIM
    n_idx = nqi + LANES
    w_main = w_in[:, :n_main].astype(BF16)
    w_idx = jnp.pad(w_in[:, n_main:], ((0, 0), (0, n_main + n_idx - w_in.shape[1])))
    wih, wil = _split(w_idx)
    wuk_t = jnp.swapaxes(w_uk, 1, 2) * (DSA_HEAD_DIM ** -0.5)
    eye = jnp.eye(DSA_HEADS, dtype=F32)
    wuk_bd = (eye[:, None, :, None] * wuk_t[:, :, None, :]).reshape(dsa_w, DSA_HEADS * kv_rank).astype(BF16)
    tm = min(ROW_TILE, s)
    row = lambda c: pl.BlockSpec((None, tm, c), lambda i, j: (i, j, 0))
    full = lambda a: pl.BlockSpec(a.shape, lambda i, j: (0,) * a.ndim)
    kvn = kv_norm.reshape(1, kv_rank)
    kern = functools.partial(_even_in_kernel, pool_w=pool_w, dsa_w=dsa_w, kv_rank=kv_rank)
    return pl.pallas_call(
        kern,
        grid=(b, s // tm),
        in_specs=[row(d), pl.BlockSpec((None, 3, d), lambda i, j: (i, 0, 0)),
                  full(w_main), full(wih), full(wil), full(kvn), full(wuk_bd)],
        out_specs=[row(pool_w), row(DSA_HEADS * kv_rank), row(kv_rank), row(nqi), row(LANES)],
        out_shape=[jax.ShapeDtypeStruct((b, s, pool_w), F32),
                   jax.ShapeDtypeStruct((b, s, DSA_HEADS * kv_rank), BF16),
                   jax.ShapeDtypeStruct((b, s, kv_rank), BF16),
                   jax.ShapeDtypeStruct((b, s, nqi), F32),
                   jax.ShapeDtypeStruct((b, s, LANES), F32)],
        compiler_params=_params("arbitrary", "arbitrary"),
    )(x, mods, w_main, wih, wil, kvn, wuk_bd)


def _pool_kernel(u_ref, halo_ref, w_ref, scale_ref, o_ref, ext_ref, *, ts):
    i = pl.program_id(1)
    ext_ref[0:POOL_HALO, :] = jnp.where(i > 0, halo_ref[...], 0.0)
    ext_ref[POOL_HALO:POOL_HALO + ts, :] = u_ref[...]
    t = i * ts + lax.broadcasted_iota(I32, (ts, 1), 0)
    for g, win in enumerate(POOL_WINDOWS):
        cols = slice(g * LANES, (g + 1) * LANES)
        cur = ext_ref[POOL_HALO:POOL_HALO + ts, cols]
        acc = cur
        for j in range(1, win):
            acc = acc + ext_ref[POOL_HALO - j:POOL_HALO - j + ts, cols]
        cnt = jnp.minimum(t + 1, win).astype(F32)
        pooled = acc / cnt - cur
        y = _dot(pooled.astype(BF16), w_ref[g]) * scale_ref[:, cols]
        o_ref[:, cols] = y.astype(BF16)


def _pool_mixer(u, pool_w, pool_scale):
    b, s, width = u.shape
    ts = min(ROW_TILE, s)
    kern = functools.partial(_pool_kernel, ts=ts)
    hb = ts // POOL_HALO
    return pl.pallas_call(
        kern,
        grid=(b, s // ts),
        in_specs=[pl.BlockSpec((None, ts, width), lambda i, j: (i, j, 0)),
                  pl.BlockSpec((None, POOL_HALO, width), lambda i, j: (i, jnp.maximum(j * hb - 1, 0), 0)),
                  pl.BlockSpec(pool_w.shape, lambda i, j: (0, 0, 0)),
                  pl.BlockSpec((1, width), lambda i, j: (0, 0))],
        out_specs=pl.BlockSpec((None, ts, width), lambda i, j: (i, j, 0)),
        out_shape=jax.ShapeDtypeStruct((b, s, width), BF16),
        scratch_shapes=[pltpu.VMEM((ts + POOL_HALO, width), F32)],
        compiler_params=_params("arbitrary", "arbitrary"),
    )(u, u, pool_w.astype(BF16), pool_scale.reshape(1, width))


LOG2E = 1.4426950408889634
DSA_QUERY_BLOCK = 256
DSA_KEY_BLOCK = 512
ONES_ROWS = 16


def _dsat_kernel(ql_ref, qi_ref, kwq_ref, kw_ref, ckv_ref, tri_ref, wuv_ref, o_ref,
                 ki3_ref, kvt_ref, keys_ref, qlt_ref, m_ref, acc_ref, *, t, sk, topk, rank):
    qb = pl.program_id(1)
    lane = lax.broadcasted_iota(I32, (1, LANES), 1)
    low = lane < IDX_DIM
    nacc = 4 * SUBLANES

    @pl.when(qb == 0)
    def _():
        k_hi, k_lo = _split(jnp.where(low, kw_ref[...], 0.0))
        hi_f = k_hi.astype(F32)
        ki3_ref[:, 0:LANES] = (hi_f + pltpu.roll(hi_f, IDX_DIM, 1)).astype(BF16)
        ki3_ref[:, LANES:2 * LANES] = k_lo
        kvt_ref[0:rank, :] = ckv_ref[...].astype(F32).T.astype(BF16)
        kvt_ref[rank:rank + ONES_ROWS, :] = jnp.ones((ONES_ROWS, kvt_ref.shape[1]), BF16)

    parts = []
    for h in range(IDX_HEADS):
        xq = qi_ref[:, (h // 2) * LANES:(h // 2 + 1) * LANES]
        if h % 2:
            xq = pltpu.roll(xq, IDX_DIM, 1)
        q_hi, q_lo = _split(jnp.where(low, xq, 0.0))
        hi_f = q_hi.astype(F32)
        a = (hi_f + pltpu.roll(q_lo.astype(F32), IDX_DIM, 1)).astype(BF16)
        parts.append(jnp.concatenate([a, q_hi], axis=1))
    qi3 = jnp.concatenate(parts, axis=0)

    kwq_t = kwq_ref[...].T
    wscale = IDX_HEADS ** -0.5 * IDX_DIM ** -0.5
    wi = [kwq_t[IDX_DIM + h:IDX_DIM + h + 1, :] * wscale for h in range(IDX_HEADS)]

    for h in range(DSA_HEADS):
        qh = ql_ref[:, h * rank:(h + 1) * rank].astype(F32).T * LOG2E
        qlt_ref[:, h * t:(h + 1) * t] = qh.astype(BF16)

    qcol = lax.broadcasted_iota(I32, (1, t), 1)
    adm = qb * t + (qcol // CHUNK + 1) * CHUNK
    nkb = (qb * t + t + sk - 1) // sk
    krow = lax.broadcasted_iota(I32, (sk, 1), 0)
    i0 = jnp.int32(0)

    def score_body(j, carry):
        off = pl.multiple_of(j * sk, sk)
        sc = _dot_nt(ki3_ref[pl.ds(off, sk), :], qi3)
        score = jnp.zeros((sk, t), F32)
        for h in range(IDX_HEADS):
            score = score + wi[h] * jnp.maximum(sc[:, h * t:(h + 1) * t], 0.0)
        score = jnp.where(score == 0.0, 0.0, score)
        bits = lax.bitcast_convert_type(score, I32)
        key = bits ^ (lax.shift_right_arithmetic(bits, jnp.int32(31)) & jnp.int32(0x7FFFFFFF))
        keys_ref[pl.ds(off, sk), :] = jnp.where(krow + off < adm, key, jnp.int32(INT_MIN))
        return carry

    lax.fori_loop(i0, nkb, score_body, 0)

    def count(pred, ref):
        def body(j, acc):
            off = pl.multiple_of(j * sk, sk)
            cols = []
            for c in range(t // LANES):
                cs = slice(c * LANES, (c + 1) * LANES)
                hit = pred(keys_ref[pl.ds(off, sk), cs], ref[:, cs]).astype(F32)
                cols.append(jnp.sum(hit.reshape(sk // nacc, nacc, LANES), axis=0))
            return acc + jnp.concatenate(cols, axis=1)
        acc = lax.fori_loop(i0, nkb, body, jnp.zeros((nacc, t), F32))
        return jnp.sum(acc, axis=0, keepdims=True)

    ge = lambda k, r: k >= r
    kf = float(topk)
    zero = jnp.zeros((1, t), I32)
    res0 = jnp.where(count(ge, zero) >= kf, zero, jnp.int32(INT_MIN))

    def bit_body(i, res):
        cand = res | lax.shift_left(jnp.int32(1), jnp.int32(30) - i)
        return jnp.where(count(ge, cand) >= kf, cand, res)

    kth = lax.fori_loop(i0, jnp.int32(31), bit_body, res0)
    need = kf - count(lambda k, r: k > r, kth)

    m_ref[...] = jnp.full(m_ref.shape, NEG_BIG, F32)
    acc_ref[...] = jnp.zeros(acc_ref.shape, F32)

    def attn_body(j, seen, ties):
        off = pl.multiple_of(j * sk, sk)
        key = keys_ref[pl.ds(off, sk), :]
        if ties:
            eq = key == kth
            rank_eq = seen + _dot(tri_ref[...], eq.astype(F32).astype(BF16))
            sel = ((key > kth) | (eq & (rank_eq <= need))) & (krow + off < adm)
        else:
            rank_eq = jnp.broadcast_to(seen, (sk, t))
            sel = (key >= kth) & (krow + off < adm)
        bias = jnp.where(sel, jnp.float32(0.0), jnp.float32(NEG_BIG))
        logits = _dot(ckv_ref[pl.ds(off, sk), :], qlt_ref[...])
        kvt = kvt_ref[:, pl.ds(off, sk)]
        for h in range(DSA_HEADS):
            lg = logits[:, h * t:(h + 1) * t] + bias
            m_old = m_ref[h:h + 1, :]
            m_new = jnp.maximum(m_old, jnp.max(lg, axis=0, keepdims=True))
            alpha = jnp.exp2(m_old - m_new)
            p = jnp.exp2(lg - m_new)
            acc_ref[h] = alpha * acc_ref[h] + _dot(kvt, p.astype(BF16))
            m_ref[h:h + 1, :] = m_new
        return rank_eq[sk - 1:sk, :]

    seen0 = jnp.zeros((1, t), F32)
    excess = jnp.max(count(lambda k, r: k == r, kth) - need) > 0.0

    @pl.when(excess)
    def _():
        lax.fori_loop(i0, nkb, functools.partial(attn_body, ties=True), seen0)

    @pl.when(jnp.logical_not(excess))
    def _():
        lax.fori_loop(i0, nkb, functools.partial(attn_body, ties=False), seen0)

    y_t = []
    for h in range(DSA_HEADS):
        o_t = acc_ref[h, 0:rank, :] / acc_ref[h, rank:rank + 1, :]
        y_t.append(_dot(wuv_ref[h], o_t.astype(BF16)))
    o_ref[...] = jnp.concatenate(y_t, axis=0).T.astype(BF16)


def _dsa_attention_t(ql, qi, kw, ckv, w_uv):
    b, s, _ = ql.shape
    rank = ckv.shape[-1]
    t = min(DSA_QUERY_BLOCK, s)
    sk = min(DSA_KEY_BLOCK, s)
    topk = min(TOPK_MAX, s // 4)
    width = DSA_HEADS * DSA_HEAD_DIM
    tri = (lax.broadcasted_iota(I32, (sk, sk), 0) >= lax.broadcasted_iota(I32, (sk, sk), 1)).astype(BF16)
    wuv = jnp.swapaxes(w_uv, 1, 2).astype(BF16)
    kern = functools.partial(_dsat_kernel, t=t, sk=sk, topk=topk, rank=rank)
    blk = lambda c: pl.BlockSpec((None, t, c), lambda i, j: (i, j, 0))
    seq = lambda c: pl.BlockSpec((None, s, c), lambda i, j: (i, 0, 0))
    return pl.pallas_call(
        kern,
        grid=(b, s // t),
        in_specs=[blk(DSA_HEADS * rank), blk(qi.shape[-1]), blk(LANES), seq(LANES), seq(rank),
                  pl.BlockSpec((sk, sk), lambda i, j: (0, 0)),
                  pl.BlockSpec(wuv.shape, lambda i, j: (0, 0, 0))],
        out_specs=blk(width),
        out_shape=jax.ShapeDtypeStruct((b, s, width), BF16),
        scratch_shapes=[pltpu.VMEM((s, 2 * LANES), BF16),
                        pltpu.VMEM((rank + ONES_ROWS, s), BF16),
                        pltpu.VMEM((s, t), I32),
                        pltpu.VMEM((rank, DSA_HEADS * t), BF16),
                        pltpu.VMEM((DSA_HEADS, t), F32),
                        pltpu.VMEM((DSA_HEADS, rank + ONES_ROWS, t), F32)],
        compiler_params=_params("arbitrary", "arbitrary"),
    )(ql, qi, kw, kw, ckv, tri, wuv)


def _resln_kernel(*refs, n_y, widths, alpha):
    x_ref, mod_ref = refs[0], refs[1]
    y_refs = refs[2:2 + n_y]
    w_ref, g_ref, b_ref, o_ref = refs[2 + n_y:]
    acc = None
    off = 0
    for y_ref, wd in zip(y_refs, widths):
        part = _dot(y_ref[...], w_ref[off:off + wd, :])
        acc = part if acc is None else acc + part
        off += wd
    z = alpha * x_ref[...] + mod_ref[2:3, :] * acc
    o_ref[...] = _layer_norm(z, g_ref[...], b_ref[...])


def _out_proj_ln(x, mods, ys, w, ln_g, ln_b, alpha):
    b, s, d = x.shape
    widths = tuple(y.shape[-1] for y in ys)
    tm = min(ROW_TILE, s)
    kern = functools.partial(_resln_kernel, n_y=len(ys), widths=widths, alpha=alpha)
    row = lambda c: pl.BlockSpec((None, tm, c), lambda i, j: (i, j, 0))
    vec = pl.BlockSpec((1, d), lambda i, j: (0, 0))
    return pl.pallas_call(
        kern,
        grid=(b, s // tm),
        in_specs=[row(d), pl.BlockSpec((None, 3, d), lambda i, j: (i, 0, 0))]
                 + [row(wd) for wd in widths]
                 + [pl.BlockSpec(w.shape, lambda i, j: (0, 0)), vec, vec],
        out_specs=row(d),
        out_shape=jax.ShapeDtypeStruct((b, s, d), F32),
        compiler_params=_params("arbitrary", "arbitrary"),
    )(x, mods, *ys, w.astype(BF16), ln_g.reshape(1, d), ln_b.reshape(1, d))


def _ffn_up_kernel(x_ref, halo_ref, mod_ref, wa_ref, wv_ref, cwa_ref, cwv_ref, cba_ref, cbv_ref,
                   o_ref, h_ref, up_ref, *, tm):
    i = pl.program_id(1)
    j = pl.program_id(2)

    @pl.when(j == 0)
    def _():
        mod = mod_ref[...]
        halo = jnp.where(i > 0, _modulate(halo_ref[...], mod), 0.0)
        h_ref[0:SUBLANES, :] = halo.astype(BF16)
        h_ref[SUBLANES:SUBLANES + tm, :] = _modulate(x_ref[...], mod).astype(BF16)

    def conv(w_ref, cw_ref, cb_ref):
        up_ref[...] = _dot(h_ref[...], w_ref[...])
        out = cb_ref[...]
        for k in range(FFN_CONV):
            lo = SUBLANES - (FFN_CONV - 1) + k
            out = out + cw_ref[k:k + 1, :] * up_ref[lo:lo + tm, :]
        return out

    a = conv(wa_ref, cwa_ref, cba_ref)
    v = conv(wv_ref, cwv_ref, cbv_ref)
    o_ref[...] = (_silu(a) * v).astype(BF16)


def _ffn_up(x, mods, w_up, conv_w, conv_b):
    b, s, d = x.shape
    d_ff = w_up.shape[1] // 2
    tm = min(ROW_TILE, s)
    tn = d_ff
    ncol = d_ff // tn
    wb = w_up.astype(BF16)
    cb = conv_b.reshape(1, 2 * d_ff)
    hb = tm // SUBLANES
    kern = functools.partial(_ffn_up_kernel, tm=tm)
    col_a = lambda r: pl.BlockSpec((r, tn), lambda i, j, k: (0, k))
    col_v = lambda r: pl.BlockSpec((r, tn), lambda i, j, k: (0, k + ncol))
    return pl.pallas_call(
        kern,
        grid=(b, s // tm, ncol),
        in_specs=[pl.BlockSpec((None, tm, d), lambda i, j, k: (i, j, 0)),
                  pl.BlockSpec((None, SUBLANES, d), lambda i, j, k: (i, jnp.maximum(j * hb - 1, 0), 0)),
                  pl.BlockSpec((None, 3, d), lambda i, j, k: (i, 0, 0)),
                  col_a(d), col_v(d), col_a(FFN_CONV), col_v(FFN_CONV), col_a(1), col_v(1)],
        out_specs=pl.BlockSpec((None, tm, tn), lambda i, j, k: (i, j, k)),
        out_shape=jax.ShapeDtypeStruct((b, s, d_ff), BF16),
        scratch_shapes=[pltpu.VMEM((tm + SUBLANES, d), BF16),
                        pltpu.VMEM((tm + SUBLANES, tn), F32)],
        compiler_params=_params("arbitrary", "arbitrary", "arbitrary"),
    )(x, x, mods, wb, wb, conv_w, conv_w, cb, cb)


def _odd_in_kernel(x_ref, mod_ref, wq_ref, wg_ref, wbh_ref, wbl_ref, cw_ref, act_ref, gate_ref, ba_ref,
                   ext_ref, *, tm, width, dh):
    j = pl.program_id(1)
    nh = width // dh
    h = _modulate(x_ref[...], mod_ref[...])
    h_hi, h_lo = _split(h)

    @pl.when(j == 0)
    def _():
        ext_ref[0:SUBLANES, :] = jnp.zeros((SUBLANES, 3 * width), F32)

    @pl.when(j > 0)
    def _():
        ext_ref[0:SUBLANES, :] = ext_ref[tm:tm + SUBLANES, :]

    ext_ref[SUBLANES:SUBLANES + tm, :] = _dot(h_hi, wq_ref[...])
    for cb in range(3 * nh):
        cols = slice(cb * dh, (cb + 1) * dh)
        acc = None
        for k in range(SHORT_CONV):
            lo = SUBLANES - (SHORT_CONV - 1) + k
            term = cw_ref[k:k + 1, cols] * ext_ref[lo:lo + tm, cols]
            acc = term if acc is None else acc + term
        z = _silu(acc)
        if cb < 2 * nh:
            z = z * lax.rsqrt(jnp.sum(z * z, axis=-1, keepdims=True) + RMS_EPS)
            if cb < nh:
                z = z * (dh ** -0.5)
        act_ref[:, cols] = z

    gate_ref[...] = _dot(h_hi, wg_ref[...])
    ba_ref[...] = _dot(h_hi, wbh_ref[...]) + _dot(h_hi, wbl_ref[...]) + _dot(h_lo, wbh_ref[...])


def _odd_in_proj(x, mods, w_in, conv_w, width):
    b, s, d = x.shape
    dh = width // GDN_HEADS
    wq = w_in[:, :3 * width].astype(BF16)
    wg = w_in[:, 3 * width:4 * width].astype(BF16)
    wba = jnp.pad(w_in[:, 4 * width:], ((0, 0), (0, LANES - 2 * GDN_HEADS)))
    wbh, wbl = _split(wba)
    tm = min(ROW_TILE, s)
    row = lambda c: pl.BlockSpec((None, tm, c), lambda i, j: (i, j, 0))
    full = lambda a: pl.BlockSpec(a.shape, lambda i, j: (0, 0))
    kern = functools.partial(_odd_in_kernel, tm=tm, width=width, dh=dh)
    return pl.pallas_call(
        kern,
        grid=(b, s // tm),
        in_specs=[row(d), pl.BlockSpec((None, 3, d), lambda i, j: (i, 0, 0)),
                  full(wq), full(wg), full(wbh), full(wbl), full(conv_w)],
        out_specs=[row(3 * width), row(width), row(LANES)],
        out_shape=[jax.ShapeDtypeStruct((b, s, 3 * width), F32),
                   jax.ShapeDtypeStruct((b, s, width), F32),
                   jax.ShapeDtypeStruct((b, s, LANES), F32)],
        scratch_shapes=[pltpu.VMEM((tm + SUBLANES, 3 * width), F32)],
        compiler_params=_params("arbitrary", "arbitrary"),
    )(x, mods, wq, wg, wbh, wbl, conv_w)


def _bmm(a, b):
    return lax.dot_general(a, b, (((2,), (1,)), ((0,), (0,))), preferred_element_type=F32)


def _bmm_nt(a, b):
    return lax.dot_general(a, b, (((2,), (2,)), ((0,), (0,))), preferred_element_type=F32)


def _unit_lower_inverse_minus_eye(a, ii, jj):
    mm = lambda p, q: _bmm(p.astype(BF16), q.astype(BF16))
    diag = (ii // INV_BLOCK) == (jj // INV_BLOCK)
    ad = jnp.where(diag, a, 0.0)
    e = a - ad
    a2 = mm(ad, ad)
    a4 = mm(a2, a2)
    a8 = mm(a4, a4)
    p = -ad
    for sq in (a2, a4, a8):
        p = p + sq + mm(p, sq)
    m = e + mm(p, e)
    m2 = mm(m, m)
    q = m2 - m - mm(m, m2)
    return q + p + mm(q, p)


def _gdn_kernel(act_ref, gate_ref, ba_ref, alog_ref, dtb_ref, onorm_ref, ltri_ref, y_ref,
                state_ref, u_ref, w_ref, attn_ref, qg_ref, kdt_ref, el_ref, *, ts, width, dh):
    i = pl.program_id(1)
    nh = width // dh

    @pl.when(i == 0)
    def _():
        state_ref[...] = jnp.zeros(state_ref.shape, F32)

    ii = lax.broadcasted_iota(I32, (CHUNK, CHUNK), 0)
    jj = lax.broadcasted_iota(I32, (CHUNK, CHUNK), 1)
    ltri = ltri_ref[...]

    def prep_body(cp, carry):
        r0 = pl.multiple_of(cp * (GDN_GROUP * CHUNK), GDN_GROUP * CHUNK)
        ba = ba_ref[pl.ds(r0, GDN_GROUP * CHUNK), :]
        beta = _sigmoid(ba)
        xg = ba + dtb_ref[...]
        softplus = jnp.maximum(xg, 0.0) + jnp.log1p(jnp.exp(-jnp.abs(xg)))
        g = -jnp.exp(alog_ref[...]) * softplus
        g1 = g.astype(BF16)
        r1 = g - g1.astype(F32)
        g2 = r1.astype(BF16)
        g3 = (r1 - g2.astype(F32)).astype(BF16)
        gc = _dot(ltri, g1) + _dot(ltri, g2) + _dot(ltri, g3)
        gc_t = gc.T
        egc = jnp.exp(gc)

        pairs = [(c, h) for c in range(GDN_GROUP) for h in range(nh)]
        rows_of = lambda c: slice(c * CHUNK, (c + 1) * CHUNK)
        col = lambda arr, lane: jnp.stack([arr[rows_of(c), lane(h):lane(h) + 1] for c, h in pairs])
        beta_c = col(beta, lambda h: h)
        gc_c = col(gc, lambda h: nh + h)
        egc_c = col(egc, lambda h: nh + h)
        gc_r = jnp.stack([gc_t[nh + h:nh + h + 1, rows_of(c)] for c, h in pairs])
        g_last = jnp.stack([gc[(c + 1) * CHUNK - 1:(c + 1) * CHUNK, nh + h:nh + h + 1] for c, h in pairs])
        act = lambda part: jnp.stack([act_ref[pl.ds(r0 + c * CHUNK, CHUNK),
                                              part * width + h * dh:part * width + (h + 1) * dh] for c, h in pairs])
        q, k, v = act(0), act(1), act(2)

        dec = jnp.exp(jnp.where(ii >= jj, gc_c - gc_r, 0.0))
        kb = k * beta_c
        k_bf = k.astype(BF16)
        a = _bmm_nt(kb.astype(BF16), k_bf) * jnp.where(ii > jj, dec, 0.0)
        attn = _bmm_nt(q.astype(BF16), k_bf) * jnp.where(ii >= jj, dec, 0.0)
        tm1 = _unit_lower_inverse_minus_eye(a, ii, jj).astype(BF16)
        vb = v * beta_c
        kbg = kb * egc_c
        n0 = pl.multiple_of(cp * (GDN_GROUP * nh), GDN_GROUP * nh)
        sl = pl.ds(n0, GDN_GROUP * nh)
        u_ref[sl] = vb + _bmm(tm1, vb.astype(BF16))
        w_ref[sl] = (kbg + _bmm(tm1, kbg.astype(BF16))).astype(BF16)
        attn_ref[sl] = attn.astype(BF16)
        qg_ref[sl] = (q * egc_c).astype(BF16)
        kdt_ref[sl] = jnp.swapaxes(k * jnp.exp(g_last - gc_c), 1, 2).astype(BF16)
        el_ref[sl] = jnp.broadcast_to(jnp.exp(g_last), (GDN_GROUP * nh, 1, dh))
        return carry

    lax.fori_loop(jnp.int32(0), jnp.int32(ts // (GDN_GROUP * CHUNK)), prep_body, 0)

    def chunk_body(c, carry):
        r0 = pl.multiple_of(c * CHUNK, CHUNK)
        sl = pl.ds(pl.multiple_of(c * nh, nh), nh)
        st = state_ref[...]
        st_bf = st.astype(BF16)
        v_new = u_ref[sl] - _bmm(w_ref[sl], st_bf)
        vn_bf = v_new.astype(BF16)
        o = _bmm(qg_ref[sl], st_bf) + _bmm(attn_ref[sl], vn_bf)
        state_ref[...] = st * el_ref[sl] + _bmm(kdt_ref[sl], vn_bf)
        on = o * lax.rsqrt(jnp.mean(o * o, axis=-1, keepdims=True) + RMS_EPS) * onorm_ref[...]
        for h in range(nh):
            hc = slice(h * dh, (h + 1) * dh)
            y_ref[pl.ds(r0, CHUNK), hc] = (on[h] * _silu(gate_ref[pl.ds(r0, CHUNK), hc])).astype(BF16)
        return carry

    lax.fori_loop(jnp.int32(0), jnp.int32(ts // CHUNK), chunk_body, 0)


def _gdn(act, gate, ba, a_log, dt_bias, out_norm):
    b, s, w3 = act.shape
    width = w3 // 3
    dh = width // GDN_HEADS
    ts = min(ROW_TILE, s)
    pad_a = lambda v: jnp.pad(v.reshape(1, GDN_HEADS), ((0, 0), (GDN_HEADS, LANES - 2 * GDN_HEADS)))
    gc_rows = GDN_GROUP * CHUNK
    ri = lax.broadcasted_iota(I32, (gc_rows, gc_rows), 0)
    ci = lax.broadcasted_iota(I32, (gc_rows, gc_rows), 1)
    ltri = ((ri >= ci) & (ri // CHUNK == ci // CHUNK)).astype(BF16)
    n_mat = (ts // CHUNK) * GDN_HEADS
    kern = functools.partial(_gdn_kernel, ts=ts, width=width, dh=dh)
    row = lambda c: pl.BlockSpec((None, ts, c), lambda i, j: (i, j, 0))
    full = lambda shape: pl.BlockSpec(shape, lambda i, j: (0,) * len(shape))
    return pl.pallas_call(
        kern,
        grid=(b, s // ts),
        in_specs=[row(w3), row(width), row(LANES), full((1, LANES)), full((1, LANES)),
                  full((1, dh)), full((gc_rows, gc_rows))],
        out_specs=row(width),
        out_shape=jax.ShapeDtypeStruct((b, s, width), BF16),
        scratch_shapes=[pltpu.VMEM((GDN_HEADS, dh, dh), F32),
                        pltpu.VMEM((n_mat, CHUNK, dh), F32),
                        pltpu.VMEM((n_mat, CHUNK, dh), BF16),
                        pltpu.VMEM((n_mat, CHUNK, CHUNK), BF16),
                        pltpu.VMEM((n_mat, CHUNK, dh), BF16),
                        pltpu.VMEM((n_mat, dh, CHUNK), BF16),
                        pltpu.VMEM((n_mat, 1, dh), F32)],
        compiler_params=_params("arbitrary", "arbitrary"),
    )(act, gate, ba, pad_a(a_log), pad_a(dt_bias), out_norm.reshape(1, dh), ltri)


def kernel(x, c, e_mod_w, e_mod_b, e_w_in, e_pool_w, e_pool_scale, e_kv_norm, e_w_uk, e_w_uv, e_w_out, e_ln_g, e_ln_b, o_mod_w, o_mod_b, o_w_in, o_conv_w, o_a_log, o_dt_bias, o_out_norm, o_w_out, o_ln_g, o_ln_b, f_mod_w, f_mod_b, f_w_up, f_conv_w, f_conv_b, f_w_down, f_ln_g, f_ln_b):
    depth = f_mod_w.shape[0]
    alpha = (2 * depth) ** 0.25
    for layer in range(depth):
        i = layer // 2
        if layer % 2 == 0:
            mods = _modulation(c, e_mod_w, e_mod_b, i)
            u, ql, ckv, qi, kw = _even_in_proj(x, mods, e_w_in[i], e_kv_norm[i], e_w_uk[i])
            y_pool = _pool_mixer(u, e_pool_w[i], e_pool_scale[i])
            y_dsa = _dsa_attention_t(ql, qi, kw, ckv, e_w_uv[i])
            x = _out_proj_ln(x, mods, [y_pool, y_dsa], e_w_out[i], e_ln_g[i], e_ln_b[i], alpha)
        else:
            mods = _modulation(c, o_mod_w, o_mod_b, i)
            width = o_w_out.shape[1]
            act, gate, ba = _odd_in_proj(x, mods, o_w_in[i], o_conv_w[i], width)
            y = _gdn(act, gate, ba, o_a_log[i], o_dt_bias[i], o_out_norm[i])
            x = _out_proj_ln(x, mods, [y], o_w_out[i], o_ln_g[i], o_ln_b[i], alpha)
        mods = _modulation(c, f_mod_w, f_mod_b, layer)
        act = _ffn_up(x, mods, f_w_up[layer], f_conv_w[layer], f_conv_b[layer])
        x = _out_proj_ln(x, mods, [act], f_w_down[layer], f_ln_g[layer], f_ln_b[layer], alpha)
    return x
```

```python
import functools

import jax
import jax.numpy as jnp
from jax import lax
from jax.experimental import pallas as pl
from jax.experimental.pallas import tpu as pltpu

F32 = jnp.float32
BF16 = jnp.bfloat16
I32 = jnp.int32

LANES = 128
SUBLANES = 8
V7X_VMEM_BYTES = 64 * 1024 * 1024
VMEM_LIMIT = V7X_VMEM_BYTES * 7 // 8

ROW_TILE = 512
MOD_COL_TILE = 512
INV_BLOCK = 16

CHUNK = 64
POOL_WINDOWS = (2, 4, 8, 16)
POOL_HALO = 16
DSA_HEADS = 8
DSA_HEAD_DIM = 64
IDX_HEADS = 4
IDX_DIM = 64
TOPK_MAX = 256
GDN_HEADS = 8
GDN_GROUP = 4
GDN_PACK = 4
SHORT_CONV = 4
FFN_CONV = 3
LN_EPS = 1e-5
RMS_EPS = 1e-6
NEG_BIG = -1e30
INT_MIN = -(2 ** 31)


def _dot(a, b):
    return jnp.dot(a, b, preferred_element_type=F32)


def _dot_nt(a, b):
    return lax.dot_general(a, b, (((1,), (1,)), ((), ())), preferred_element_type=F32)


def _split(a):
    hi = a.astype(BF16)
    lo = (a - hi.astype(F32)).astype(BF16)
    return hi, lo


def _sigmoid(x):
    return 1.0 / (1.0 + jnp.exp(-x))


def _silu(x):
    return x * _sigmoid(x)


def _layer_norm(z, g, b):
    mu = jnp.mean(z, axis=-1, keepdims=True)
    zc = z - mu
    var = jnp.mean(zc * zc, axis=-1, keepdims=True)
    return zc * lax.rsqrt(var + LN_EPS) * g + b


def _params(*sem):
    return pltpu.CompilerParams(dimension_semantics=sem, vmem_limit_bytes=VMEM_LIMIT)


def _mod_kernel(c_ref, w_ref, b_ref, o_ref):
    c = c_ref[...]
    a_hi, a_lo = _split(_silu(c))
    w_hi, w_lo = _split(w_ref[...])
    o_ref[...] = _dot(a_hi, w_hi) + _dot(a_hi, w_lo) + _dot(a_lo, w_hi) + b_ref[...]


def _modulation(c, mod_w, mod_b, layer):
    b, d = c.shape
    n3 = mod_w.shape[-1]
    tn = MOD_COL_TILE
    bias = mod_b.reshape(mod_b.shape[0], 1, n3)
    out = pl.pallas_call(
        _mod_kernel,
        grid=(n3 // tn,),
        in_specs=[pl.BlockSpec((b, d), lambda j: (0, 0)),
                  pl.BlockSpec((None, d, tn), lambda j: (layer, 0, j)),
                  pl.BlockSpec((None, 1, tn), lambda j: (layer, 0, j))],
        out_specs=pl.BlockSpec((b, tn), lambda j: (0, j)),
        out_shape=jax.ShapeDtypeStruct((b, n3), F32),
        compiler_params=_params("arbitrary"),
    )(c, mod_w, bias)
    return out.reshape(b, 3, d)


def _modulate(x, mod):
    return x * (1.0 + mod[1:2, :]) + mod[0:1, :]


def _even_in_kernel(x_ref, mod_ref, wm_ref, wih_ref, wil_ref, kvn_ref, wuk_ref,
                    u_ref, ql_ref, ckv_ref, qi_ref, kw_ref, *, pool_w, dsa_w, kv_rank):
    h = _modulate(x_ref[...], mod_ref[...])
    h_hi, h_lo = _split(h)
    main = _dot(h_hi, wm_ref[...])
    u_ref[...] = main[:, :pool_w]
    q = main[:, pool_w:pool_w + dsa_w]
    ql_ref[...] = _dot(q.astype(BF16), wuk_ref[...]).astype(BF16)
    ckv = main[:, pool_w + dsa_w:]
    ms = jnp.mean(ckv * ckv, axis=-1, keepdims=True)
    ckv_ref[...] = (ckv * lax.rsqrt(ms + RMS_EPS) * kvn_ref[...]).astype(BF16)
    idx = _dot(h_hi, wih_ref[...]) + _dot(h_hi, wil_ref[...]) + _dot(h_lo, wih_ref[...])
    nqi = IDX_HEADS * IDX_DIM
    qi_ref[...] = idx[:, :nqi]
    kw_ref[...] = idx[:, nqi:]


def _even_in_proj(x, mods, w_in, kv_norm, w_uk):
    b, s, d = x.shape
    pool_w = len(POOL_WINDOWS) * LANES
    dsa_w = DSA_HEADS * DSA_HEAD_DIM
    kv_rank = w_uk.shape[1]
    n_main = pool_w + dsa_w + kv_rank
    nqi = IDX_HEADS * IDX_DIM
    n_idx = nqi + LANES
    w_main = w_in[:, :n_main].astype(BF16)
    w_idx = jnp.pad(w_in[:, n_main:], ((0, 0), (0, n_main + n_idx - w_in.shape[1])))
    wih, wil = _split(w_idx)
    wuk_t = jnp.swapaxes(w_uk, 1, 2) * (DSA_HEAD_DIM ** -0.5)
    eye = jnp.eye(DSA_HEADS, dtype=F32)
    wuk_bd = (eye[:, None, :, None] * wuk_t[:, :, None, :]).reshape(dsa_w, DSA_HEADS * kv_rank).astype(BF16)
    tm = min(ROW_TILE, s)
    row = lambda c: pl.BlockSpec((None, tm, c), lambda i, j: (i, j, 0))
    full = lambda a: pl.BlockSpec(a.shape, lambda i, j: (0,) * a.ndim)
    kvn = kv_norm.reshape(1, kv_rank)
    kern = functools.partial(_even_in_kernel, pool_w=pool_w, dsa_w=dsa_w, kv_rank=kv_rank)
    return pl.pallas_call(
        kern,
        grid=(b, s // tm),
        in_specs=[row(d), pl.BlockSpec((None, 3, d), lambda i, j: (i, 0, 0)),
                  full(w_main), full(wih), full(wil), full(kvn), full(wuk_bd)],
        out_specs=[row(pool_w), row(DSA_HEADS * kv_rank), row(kv_rank), row(nqi), row(LANES)],
        out_shape=[jax.ShapeDtypeStruct((b, s, pool_w), F32),
                   jax.ShapeDtypeStruct((b, s, DSA_HEADS * kv_rank), BF16),
                   jax.ShapeDtypeStruct((b, s, kv_rank), BF16),
                   jax.ShapeDtypeStruct((b, s, nqi), F32),
                   jax.ShapeDtypeStruct((b, s, LANES), F32)],
        compiler_params=_params("arbitrary", "arbitrary"),
    )(x, mods, w_main, wih, wil, kvn, wuk_bd)


def _pool_kernel(u_ref, halo_ref, w_ref, scale_ref, o_ref, ext_ref, *, ts):
    i = pl.program_id(1)
    ext_ref[0:POOL_HALO, :] = jnp.where(i > 0, halo_ref[...], 0.0)
    ext_ref[POOL_HALO:POOL_HALO + ts, :] = u_ref[...]
    t = i * ts + lax.broadcasted_iota(I32, (ts, 1), 0)
    for g, win in enumerate(POOL_WINDOWS):
        cols = slice(g * LANES, (g + 1) * LANES)
        cur = ext_ref[POOL_HALO:POOL_HALO + ts, cols]
        acc = cur
        for j in range(1, win):
            acc = acc + ext_ref[POOL_HALO - j:POOL_HALO - j + ts, cols]
        cnt = jnp.minimum(t + 1, win).astype(F32)
        pooled = acc / cnt - cur
        y = _dot(pooled.astype(BF16), w_ref[g]) * scale_ref[:, cols]
        o_ref[:, cols] = y.astype(BF16)


def _pool_mixer(u, pool_w, pool_scale):
    b, s, width = u.shape
    ts = min(ROW_TILE, s)
    kern = functools.partial(_pool_kernel, ts=ts)
    hb = ts // POOL_HALO
    return pl.pallas_call(
        kern,
        grid=(b, s // ts),
        in_specs=[pl.BlockSpec((None, ts, width), lambda i, j: (i, j, 0)),
                  pl.BlockSpec((None, POOL_HALO, width), lambda i, j: (i, jnp.maximum(j * hb - 1, 0), 0)),
                  pl.BlockSpec(pool_w.shape, lambda i, j: (0, 0, 0)),
                  pl.BlockSpec((1, width), lambda i, j: (0, 0))],
        out_specs=pl.BlockSpec((None, ts, width), lambda i, j: (i, j, 0)),
        out_shape=jax.ShapeDtypeStruct((b, s, width), BF16),
        scratch_shapes=[pltpu.VMEM((ts + POOL_HALO, width), F32)],
        compiler_params=_params("arbitrary", "arbitrary"),
    )(u, u, pool_w.astype(BF16), pool_scale.reshape(1, width))


LOG2E = 1.4426950408889634
DSA_QUERY_BLOCK = 256
DSA_KEY_BLOCK = 512
ONES_ROWS = 16


def _dsat_kernel(ql_ref, qi_ref, kwq_ref, kw_ref, ckv_ref, tri_ref, wuv_ref, o_ref,
                 ki3_ref, kvt_ref, keys_ref, qlt_ref, m_ref, acc_ref, *, t, sk, topk, rank):
    qb = pl.program_id(1)
    lane = lax.broadcasted_iota(I32, (1, LANES), 1)
    low = lane < IDX_DIM
    nacc = 4 * SUBLANES

    @pl.when(qb == 0)
    def _():
        k_hi, k_lo = _split(jnp.where(low, kw_ref[...], 0.0))
        hi_f = k_hi.astype(F32)
        ki3_ref[:, 0:LANES] = (hi_f + pltpu.roll(hi_f, IDX_DIM, 1)).astype(BF16)
        ki3_ref[:, LANES:2 * LANES] = k_lo
        kvt_ref[0:rank, :] = ckv_ref[...].astype(F32).T.astype(BF16)
        kvt_ref[rank:rank + ONES_ROWS, :] = jnp.ones((ONES_ROWS, kvt_ref.shape[1]), BF16)

    parts = []
    for h in range(IDX_HEADS):
        xq = qi_ref[:, (h // 2) * LANES:(h // 2 + 1) * LANES]
        if h % 2:
            xq = pltpu.roll(xq, IDX_DIM, 1)
        q_hi, q_lo = _split(jnp.where(low, xq, 0.0))
        hi_f = q_hi.astype(F32)
        a = (hi_f + pltpu.roll(q_lo.astype(F32), IDX_DIM, 1)).astype(BF16)
        parts.append(jnp.concatenate([a, q_hi], axis=1))
    qi3 = jnp.concatenate(parts, axis=0)

    kwq_t = kwq_ref[...].T
    wscale = IDX_HEADS ** -0.5 * IDX_DIM ** -0.5
    wi = [kwq_t[IDX_DIM + h:IDX_DIM + h + 1, :] * wscale for h in range(IDX_HEADS)]

    for h in range(DSA_HEADS):
        qh = ql_ref[:, h * rank:(h + 1) * rank].astype(F32).T * LOG2E
        qlt_ref[:, h * t:(h + 1) * t] = qh.astype(BF16)

    qcol = lax.broadcasted_iota(I32, (1, t), 1)
    adm = qb * t + (qcol // CHUNK + 1) * CHUNK
    nkb = (qb * t + t + sk - 1) // sk
    krow = lax.broadcasted_iota(I32, (sk, 1), 0)
    i0 = jnp.int32(0)

    def score_body(j, carry):
        off = pl.multiple_of(j * sk, sk)
        sc = _dot_nt(ki3_ref[pl.ds(off, sk), :], qi3)
        score = jnp.zeros((sk, t), F32)
        for h in range(IDX_HEADS):
            score = score + wi[h] * jnp.maximum(sc[:, h * t:(h + 1) * t], 0.0)
        score = jnp.where(score == 0.0, 0.0, score)
        bits = lax.bitcast_convert_type(score, I32)
        key = bits ^ (lax.shift_right_arithmetic(bits, jnp.int32(31)) & jnp.int32(0x7FFFFFFF))
        keys_ref[pl.ds(off, sk), :] = jnp.where(krow + off < adm, key, jnp.int32(INT_MIN))
        return carry

    lax.fori_loop(i0, nkb, score_body, 0)

    def count(pred, ref):
        def body(j, acc):
            off = pl.multiple_of(j * sk, sk)
            cols = []
            for c in range(t // LANES):
                cs = slice(c * LANES, (c + 1) * LANES)
                hit = pred(keys_ref[pl.ds(off, sk), cs], ref[:, cs]).astype(F32)
                cols.append(jnp.sum(hit.reshape(sk // nacc, nacc, LANES), axis=0))
            return acc + jnp.concatenate(cols, axis=1)
        acc = lax.fori_loop(i0, nkb, body, jnp.zeros((nacc, t), F32))
        return jnp.sum(acc, axis=0, keepdims=True)

    ge = lambda k, r: k >= r
    kf = float(topk)
    zero = jnp.zeros((1, t), I32)
    res0 = jnp.where(count(ge, zero) >= kf, zero, jnp.int32(INT_MIN))

    def bit_body(i, res):
        cand = res | lax.shift_left(jnp.int32(1), jnp.int32(30) - i)
        return jnp.where(count(ge, cand) >= kf, cand, res)

    kth = lax.fori_loop(i0, jnp.int32(31), bit_body, res0)
    need = kf - count(lambda k, r: k > r, kth)

    m_ref[...] = jnp.full(m_ref.shape, NEG_BIG, F32)
    acc_ref[...] = jnp.zeros(acc_ref.shape, F32)

    def attn_body(j, seen, ties):
        off = pl.multiple_of(j * sk, sk)
        key = keys_ref[pl.ds(off, sk), :]
        if ties:
            eq = key == kth
            rank_eq = seen + _dot(tri_ref[...], eq.astype(F32).astype(BF16))
            sel = ((key > kth) | (eq & (rank_eq <= need))) & (krow + off < adm)
        else:
            rank_eq = jnp.broadcast_to(seen, (sk, t))
            sel = (key >= kth) & (krow + off < adm)
        bias = jnp.where(sel, jnp.float32(0.0), jnp.float32(NEG_BIG))
        logits = _dot(ckv_ref[pl.ds(off, sk), :], qlt_ref[...])
        kvt = kvt_ref[:, pl.ds(off, sk)]
        for h in range(DSA_HEADS):
            lg = logits[:, h * t:(h + 1) * t] + bias
            m_old = m_ref[h:h + 1, :]
            m_new = jnp.maximum(m_old, jnp.max(lg, axis=0, keepdims=True))
            alpha = jnp.exp2(m_old - m_new)
            p = jnp.exp2(lg - m_new)
            acc_ref[h] = alpha * acc_ref[h] + _dot(kvt, p.astype(BF16))
            m_ref[h:h + 1, :] = m_new
        return rank_eq[sk - 1:sk, :]

    seen0 = jnp.zeros((1, t), F32)
    excess = jnp.max(count(lambda k, r: k == r, kth) - need) > 0.0

    @pl.when(excess)
    def _():
        lax.fori_loop(i0, nkb, functools.partial(attn_body, ties=True), seen0)

    @pl.when(jnp.logical_not(excess))
    def _():
        lax.fori_loop(i0, nkb, functools.partial(attn_body, ties=False), seen0)

    y_t = []
    for h in range(DSA_HEADS):
        o_t = acc_ref[h, 0:rank, :] / acc_ref[h, rank:rank + 1, :]
        y_t.append(_dot(wuv_ref[h], o_t.astype(BF16)))
    o_ref[...] = jnp.concatenate(y_t, axis=0).T.astype(BF16)


def _dsa_attention_t(ql, qi, kw, ckv, w_uv):
    b, s, _ = ql.shape
    rank = ckv.shape[-1]
    t = min(DSA_QUERY_BLOCK, s)
    sk = min(DSA_KEY_BLOCK, s)
    topk = min(TOPK_MAX, s // 4)
    width = DSA_HEADS * DSA_HEAD_DIM
    tri = (lax.broadcasted_iota(I32, (sk, sk), 0) >= lax.broadcasted_iota(I32, (sk, sk), 1)).astype(BF16)
    wuv = jnp.swapaxes(w_uv, 1, 2).astype(BF16)
    kern = functools.partial(_dsat_kernel, t=t, sk=sk, topk=topk, rank=rank)
    blk = lambda c: pl.BlockSpec((None, t, c), lambda i, j: (i, j, 0))
    seq = lambda c: pl.BlockSpec((None, s, c), lambda i, j: (i, 0, 0))
    return pl.pallas_call(
        kern,
        grid=(b, s // t),
        in_specs=[blk(DSA_HEADS * rank), blk(qi.shape[-1]), blk(LANES), seq(LANES), seq(rank),
                  pl.BlockSpec((sk, sk), lambda i, j: (0, 0)),
                  pl.BlockSpec(wuv.shape, lambda i, j: (0, 0, 0))],
        out_specs=blk(width),
        out_shape=jax.ShapeDtypeStruct((b, s, width), BF16),
        scratch_shapes=[pltpu.VMEM((s, 2 * LANES), BF16),
                        pltpu.VMEM((rank + ONES_ROWS, s), BF16),
                        pltpu.VMEM((s, t), I32),
                        pltpu.VMEM((rank, DSA_HEADS * t), BF16),
                        pltpu.VMEM((DSA_HEADS, t), F32),
                        pltpu.VMEM((DSA_HEADS, rank + ONES_ROWS, t), F32)],
        compiler_params=_params("arbitrary", "arbitrary"),
    )(ql, qi, kw, kw, ckv, tri, wuv)


def _resln_kernel(*refs, n_y, widths, alpha):
    x_ref, mod_ref = refs[0], refs[1]
    y_refs = refs[2:2 + n_y]
    w_ref, g_ref, b_ref, o_ref = refs[2 + n_y:]
    acc = None
    off = 0
    for y_ref, wd in zip(y_refs, widths):
        part = _dot(y_ref[...], w_ref[off:off + wd, :])
        acc = part if acc is None else acc + part
        off += wd
    z = alpha * x_ref[...] + mod_ref[2:3, :] * acc
    o_ref[...] = _layer_norm(z, g_ref[...], b_ref[...])


def _out_proj_ln(x, mods, ys, w, ln_g, ln_b, alpha):
    b, s, d = x.shape
    widths = tuple(y.shape[-1] for y in ys)
    tm = min(ROW_TILE, s)
    kern = functools.partial(_resln_kernel, n_y=len(ys), widths=widths, alpha=alpha)
    row = lambda c: pl.BlockSpec((None, tm, c), lambda i, j: (i, j, 0))
    vec = pl.BlockSpec((1, d), lambda i, j: (0, 0))
    return pl.pallas_call(
        kern,
        grid=(b, s // tm),
        in_specs=[row(d), pl.BlockSpec((None, 3, d), lambda i, j: (i, 0, 0))]
                 + [row(wd) for wd in widths]
                 + [pl.BlockSpec(w.shape, lambda i, j: (0, 0)), vec, vec],
        out_specs=row(d),
        out_shape=jax.ShapeDtypeStruct((b, s, d), F32),
        compiler_params=_params("arbitrary", "arbitrary"),
    )(x, mods, *ys, w.astype(BF16), ln_g.reshape(1, d), ln_b.reshape(1, d))


def _ffn_up_kernel(x_ref, halo_ref, mod_ref, wa_ref, wv_ref, cwa_ref, cwv_ref, cba_ref, cbv_ref,
                   o_ref, h_ref, up_ref, *, tm):
    i = pl.program_id(1)
    j = pl.program_id(2)

    @pl.when(j == 0)
    def _():
        mod = mod_ref[...]
        halo = jnp.where(i > 0, _modulate(halo_ref[...], mod), 0.0)
        h_ref[0:SUBLANES, :] = halo.astype(BF16)
        h_ref[SUBLANES:SUBLANES + tm, :] = _modulate(x_ref[...], mod).astype(BF16)

    def conv(w_ref, cw_ref, cb_ref):
        up_ref[...] = _dot(h_ref[...], w_ref[...])
        out = cb_ref[...]
        for k in range(FFN_CONV):
            lo = SUBLANES - (FFN_CONV - 1) + k
            out = out + cw_ref[k:k + 1, :] * up_ref[lo:lo + tm, :]
        return out

    a = conv(wa_ref, cwa_ref, cba_ref)
    v = conv(wv_ref, cwv_ref, cbv_ref)
    o_ref[...] = (_silu(a) * v).astype(BF16)


def _ffn_up(x, mods, w_up, conv_w, conv_b):
    b, s, d = x.shape
    d_ff = w_up.shape[1] // 2
    tm = min(ROW_TILE, s)
    tn = d_ff
    ncol = d_ff // tn
    wb = w_up.astype(BF16)
    cb = conv_b.reshape(1, 2 * d_ff)
    hb = tm // SUBLANES
    kern = functools.partial(_ffn_up_kernel, tm=tm)
    col_a = lambda r: pl.BlockSpec((r, tn), lambda i, j, k: (0, k))
    col_v = lambda r: pl.BlockSpec((r, tn), lambda i, j, k: (0, k + ncol))
    return pl.pallas_call(
        kern,
        grid=(b, s // tm, ncol),
        in_specs=[pl.BlockSpec((None, tm, d), lambda i, j, k: (i, j, 0)),
                  pl.BlockSpec((None, SUBLANES, d), lambda i, j, k: (i, jnp.maximum(j * hb - 1, 0), 0)),
                  pl.BlockSpec((None, 3, d), lambda i, j, k: (i, 0, 0)),
                  col_a(d), col_v(d), col_a(FFN_CONV), col_v(FFN_CONV), col_a(1), col_v(1)],
        out_specs=pl.BlockSpec((None, tm, tn), lambda i, j, k: (i, j, k)),
        out_shape=jax.ShapeDtypeStruct((b, s, d_ff), BF16),
        scratch_shapes=[pltpu.VMEM((tm + SUBLANES, d), BF16),
                        pltpu.VMEM((tm + SUBLANES, tn), F32)],
        compiler_params=_params("arbitrary", "arbitrary", "arbitrary"),
    )(x, x, mods, wb, wb, conv_w, conv_w, cb, cb)


def _odd_in_kernel(x_ref, mod_ref, wq_ref, wg_ref, wbh_ref, wbl_ref, cw_ref, act_ref, gate_ref, ba_ref,
                   ext_ref, *, tm, width, dh):
    j = pl.program_id(1)
    nh = width // dh
    h = _modulate(x_ref[...], mod_ref[...])
    h_hi, h_lo = _split(h)

    @pl.when(j == 0)
    def _():
        ext_ref[0:SUBLANES, :] = jnp.zeros((SUBLANES, 3 * width), F32)

    @pl.when(j > 0)
    def _():
        ext_ref[0:SUBLANES, :] = ext_ref[tm:tm + SUBLANES, :]

    ext_ref[SUBLANES:SUBLANES + tm, :] = _dot(h_hi, wq_ref[...])
    for cb in range(3 * nh):
        cols = slice(cb * dh, (cb + 1) * dh)
        acc = None
        for k in range(SHORT_CONV):
            lo = SUBLANES - (SHORT_CONV - 1) + k
            term = cw_ref[k:k + 1, cols] * ext_ref[lo:lo + tm, cols]
            acc = term if acc is None else acc + term
        z = _silu(acc)
        if cb < 2 * nh:
            z = z * lax.rsqrt(jnp.sum(z * z, axis=-1, keepdims=True) + RMS_EPS)
            if cb < nh:
                z = z * (dh ** -0.5)
        act_ref[:, cols] = z

    gate_ref[...] = _dot(h_hi, wg_ref[...])
    ba_ref[...] = _dot(h_hi, wbh_ref[...]) + _dot(h_hi, wbl_ref[...]) + _dot(h_lo, wbh_ref[...])


def _odd_in_proj(x, mods, w_in, conv_w, width):
    b, s, d = x.shape
    dh = width // GDN_HEADS
    wq = w_in[:, :3 * width].astype(BF16)
    wg = w_in[:, 3 * width:4 * width].astype(BF16)
    wba = jnp.pad(w_in[:, 4 * width:], ((0, 0), (0, LANES - 2 * GDN_HEADS)))
    wbh, wbl = _split(wba)
    tm = min(ROW_TILE, s)
    row = lambda c: pl.BlockSpec((None, tm, c), lambda i, j: (i, j, 0))
    full = lambda a: pl.BlockSpec(a.shape, lambda i, j: (0, 0))
    kern = functools.partial(_odd_in_kernel, tm=tm, width=width, dh=dh)
    return pl.pallas_call(
        kern,
        grid=(b, s // tm),
        in_specs=[row(d), pl.BlockSpec((None, 3, d), lambda i, j: (i, 0, 0)),
                  full(wq), full(wg), full(wbh), full(wbl), full(conv_w)],
        out_specs=[row(3 * width), row(width), row(LANES)],
        out_shape=[jax.ShapeDtypeStruct((b, s, 3 * width), F32),
                   jax.ShapeDtypeStruct((b, s, width), F32),
                   jax.ShapeDtypeStruct((b, s, LANES), F32)],
        scratch_shapes=[pltpu.VMEM((tm + SUBLANES, 3 * width), F32)],
        compiler_params=_params("arbitrary", "arbitrary"),
    )(x, mods, wq, wg, wbh, wbl, conv_w)


def _bmm(a, b):
    return lax.dot_general(a, b, (((2,), (1,)), ((0,), (0,))), preferred_element_type=F32)


def _bmm_nt(a, b):
    return lax.dot_general(a, b, (((2,), (2,)), ((0,), (0,))), preferred_element_type=F32)


def _unit_lower_inverse_minus_eye(a, ii, jj, mm):
    diag = (ii // INV_BLOCK) == (jj // INV_BLOCK)
    ad = jnp.where(diag, a, 0.0)
    e = a - ad
    a2 = mm(ad, ad)
    a4 = mm(a2, a2)
    a8 = mm(a4, a4)
    p = -ad
    for sq in (a2, a4, a8):
        p = p + sq + mm(p, sq)
    m = e + mm(p, e)
    m2 = mm(m, m)
    q = m2 - m - mm(m, m2)
    return q + p + mm(q, p)


def _gdn_kernel(act_ref, gate_ref, ba_ref, alog_ref, dtb_ref, onorm_ref, ltri_ref, y_ref,
                state_ref, u_ref, w_ref, attn_ref, qg_ref, kdt_ref, el_ref, *, ts, width, dh):
    i = pl.program_id(1)
    nh = width // dh

    @pl.when(i == 0)
    def _():
        state_ref[...] = jnp.zeros(state_ref.shape, F32)

    pc = GDN_PACK * CHUNK
    ii = lax.broadcasted_iota(I32, (CHUNK, pc), 0)
    lane_c = lax.broadcasted_iota(I32, (CHUNK, pc), 1)
    jj = lane_c - (lane_c // CHUNK) * CHUNK
    blk_row = lax.broadcasted_iota(I32, (pc, 1), 0) // CHUNK
    mask_w = blk_row == lax.broadcasted_iota(I32, (1, GDN_PACK * dh), 1) // dh
    mask_c = blk_row == lax.broadcasted_iota(I32, (1, pc), 1) // CHUNK
    ltri = ltri_ref[...]

    def prep_body(cp, carry):
        r0 = pl.multiple_of(cp * (GDN_GROUP * CHUNK), GDN_GROUP * CHUNK)
        ba = ba_ref[pl.ds(r0, GDN_GROUP * CHUNK), :]
        beta = _sigmoid(ba)
        xg = ba + dtb_ref[...]
        softplus = jnp.maximum(xg, 0.0) + jnp.log1p(jnp.exp(-jnp.abs(xg)))
        g = -jnp.exp(alog_ref[...]) * softplus
        g1 = g.astype(BF16)
        r1 = g - g1.astype(F32)
        g2 = r1.astype(BF16)
        g3 = (r1 - g2.astype(F32)).astype(BF16)
        gc = _dot(ltri, g1) + _dot(ltri, g2) + _dot(ltri, g3)
        gc_t = gc.T
        egc = jnp.exp(gc)

        pairs = [(c, h) for c in range(GDN_GROUP) for h in range(nh)]
        rows_of = lambda c: slice(c * CHUNK, (c + 1) * CHUNK)
        col = lambda arr, lane: jnp.stack([arr[rows_of(c), lane(h):lane(h) + 1] for c, h in pairs])
        beta_t = beta.T
        egc_c = col(egc, lambda h: nh + h)
        act = lambda part: jnp.stack([act_ref[pl.ds(r0 + c * CHUNK, CHUNK),
                                              part * width + h * dh:part * width + (h + 1) * dh] for c, h in pairs])
        q, k, v = act(0), act(1), act(2)

        groups = [(c, hg) for c in range(GDN_GROUP) for hg in range(nh // GDN_PACK)]
        heads_of = lambda c, hg: [c * nh + hg * GDN_PACK + j for j in range(GDN_PACK)]
        pack = lambda x: jnp.stack([jnp.concatenate([x[n] for n in heads_of(c, hg)], axis=1) for c, hg in groups])
        bd = lambda x, mask: jnp.where(mask, jnp.concatenate([x] * GDN_PACK, axis=1), 0.0).astype(BF16)
        col_p = lambda arr, lane0: jnp.stack([jnp.concatenate(
            [jnp.broadcast_to(arr[rows_of(c), lane0 + hg * GDN_PACK + j:lane0 + hg * GDN_PACK + j + 1], (CHUNK, CHUNK))
             for j in range(GDN_PACK)], axis=1) for c, hg in groups])
        row_p = lambda arr_t, lane0: jnp.stack([jnp.concatenate(
            [arr_t[lane0 + hg * GDN_PACK + j:lane0 + hg * GDN_PACK + j + 1, rows_of(c)]
             for j in range(GDN_PACK)], axis=1) for c, hg in groups])
        gc_rp = row_p(gc_t, nh)
        dec = jnp.exp(jnp.where(ii >= jj, col_p(gc, nh) - gc_rp, 0.0))
        k_bd = bd(pack(k), mask_w)
        k_p = pack(k).astype(BF16)
        a = _bmm_nt(k_p, k_bd) * (col_p(beta, 0) * jnp.where(ii > jj, dec, 0.0))
        attn = _bmm_nt(pack(q).astype(BF16), k_bd) * jnp.where(ii >= jj, dec, 0.0)
        mm = lambda x, y: _bmm(x.astype(BF16), bd(y, mask_c))
        t_full = _unit_lower_inverse_minus_eye(a, ii, jj, mm) + jnp.where(ii == jj, 1.0, 0.0).astype(F32)
        t_beta = t_full * row_p(beta_t, 0)
        u_p = _bmm(t_beta.astype(BF16), bd(pack(v), mask_w))
        w_p = _bmm((t_beta * jnp.exp(gc_rp)).astype(BF16), k_bd)
        unpack = lambda x, wd: jnp.stack([x[m][:, j * wd:(j + 1) * wd]
                                          for m in range(len(groups)) for j in range(GDN_PACK)])
        n0 = pl.multiple_of(cp * (GDN_GROUP * nh), GDN_GROUP * nh)
        sl = pl.ds(n0, GDN_GROUP * nh)
        u_ref[sl] = unpack(u_p, dh)
        w_ref[sl] = unpack(w_p, dh).astype(BF16)
        attn_ref[sl] = unpack(attn, CHUNK).astype(BF16)
        qg_ref[sl] = (q * egc_c).astype(BF16)
        gc_r = jnp.stack([gc_t[nh + h:nh + h + 1, rows_of(c)] for c, h in pairs])
        g_last = gc_r[:, :, CHUNK - 1:CHUNK]
        kdt_ref[sl] = (jnp.swapaxes(k, 1, 2) * jnp.exp(g_last - gc_r)).astype(BF16)
        el_ref[sl] = jnp.broadcast_to(jnp.exp(g_last), (GDN_GROUP * nh, 1, dh))
        return carry

    lax.fori_loop(jnp.int32(0), jnp.int32(ts // (GDN_GROUP * CHUNK)), prep_body, 0)

    def chunk_body(c, carry):
        r0 = pl.multiple_of(c * CHUNK, CHUNK)
        sl = pl.ds(pl.multiple_of(c * nh, nh), nh)
        st = state_ref[...]
        st_bf = st.astype(BF16)
        v_new = u_ref[sl] - _bmm(w_ref[sl], st_bf)
        vn_bf = v_new.astype(BF16)
        o = _bmm(qg_ref[sl], st_bf) + _bmm(attn_ref[sl], vn_bf)
        state_ref[...] = st * el_ref[sl] + _bmm(kdt_ref[sl], vn_bf)
        on = o * lax.rsqrt(jnp.mean(o * o, axis=-1, keepdims=True) + RMS_EPS) * onorm_ref[...]
        for h in range(nh):
            hc = slice(h * dh, (h + 1) * dh)
            y_ref[pl.ds(r0, CHUNK), hc] = (on[h] * _silu(gate_ref[pl.ds(r0, CHUNK), hc])).astype(BF16)
        return carry

    lax.fori_loop(jnp.int32(0), jnp.int32(ts // CHUNK), chunk_body, 0)


def _gdn(act, gate, ba, a_log, dt_bias, out_norm):
    b, s, w3 = act.shape
    width = w3 // 3
    dh = width // GDN_HEADS
    ts = min(ROW_TILE, s)
    pad_a = lambda v: jnp.pad(v.reshape(1, GDN_HEADS), ((0, 0), (GDN_HEADS, LANES - 2 * GDN_HEADS)))
    gc_rows = GDN_GROUP * CHUNK
    ri = lax.broadcasted_iota(I32, (gc_rows, gc_rows), 0)
    ci = lax.broadcasted_iota(I32, (gc_rows, gc_rows), 1)
    ltri = ((ri >= ci) & (ri // CHUNK == ci // CHUNK)).astype(BF16)
    n_mat = (ts // CHUNK) * GDN_HEADS
    kern = functools.partial(_gdn_kernel, ts=ts, width=width, dh=dh)
    row = lambda c: pl.BlockSpec((None, ts, c), lambda i, j: (i, j, 0))
    full = lambda shape: pl.BlockSpec(shape, lambda i, j: (0,) * len(shape))
    return pl.pallas_call(
        kern,
        grid=(b, s // ts),
        in_specs=[row(w3), row(width), row(LANES), full((1, LANES)), full((1, LANES)),
                  full((1, dh)), full((gc_rows, gc_rows))],
        out_specs=row(width),
        out_shape=jax.ShapeDtypeStruct((b, s, width), BF16),
        scratch_shapes=[pltpu.VMEM((GDN_HEADS, dh, dh), F32),
                        pltpu.VMEM((n_mat, CHUNK, dh), F32),
                        pltpu.VMEM((n_mat, CHUNK, dh), BF16),
                        pltpu.VMEM((n_mat, CHUNK, CHUNK), BF16),
                        pltpu.VMEM((n_mat, CHUNK, dh), BF16),
                        pltpu.VMEM((n_mat, dh, CHUNK), BF16),
                        pltpu.VMEM((n_mat, 1, dh), F32)],
        compiler_params=_params("arbitrary", "arbitrary"),
    )(act, gate, ba, pad_a(a_log), pad_a(dt_bias), out_norm.reshape(1, dh), ltri)


def kernel(x, c, e_mod_w, e_mod_b, e_w_in, e_pool_w, e_pool_scale, e_kv_norm, e_w_uk, e_w_uv, e_w_out, e_ln_g, e_ln_b, o_mod_w, o_mod_b, o_w_in, o_conv_w, o_a_log, o_dt_bias, o_out_norm, o_w_out, o_ln_g, o_ln_b, f_mod_w, f_mod_b, f_w_up, f_conv_w, f_conv_b, f_w_down, f_ln_g, f_ln_b):
    depth = f_mod_w.shape[0]
    alpha = (2 * depth) ** 0.25
    for layer in range(depth):
        i = layer // 2
        if layer % 2 == 0:
            mods = _modulation(c, e_mod_w, e_mod_b, i)
            u, ql, ckv, qi, kw = _even_in_proj(x, mods, e_w_in[i], e_kv_norm[i], e_w_uk[i])
            y_pool = _pool_mixer(u, e_pool_w[i], e_pool_scale[i])
            y_dsa = _dsa_attention_t(ql, qi, kw, ckv, e_w_uv[i])
            x = _out_proj_ln(x, mods, [y_pool, y_dsa], e_w_out[i], e_ln_g[i], e_ln_b[i], alpha)
        else:
            mods = _modulation(c, o_mod_w, o_mod_b, i)
            width = o_w_out.shape[1]
            act, gate, ba = _odd_in_proj(x, mods, o_w_in[i], o_conv_w[i], width)
            y = _gdn(act, gate, ba, o_a_log[i], o_dt_bias[i], o_out_norm[i])
            x = _out_proj_ln(x, mods, [y], o_w_out[i], o_ln_g[i], o_ln_b[i], alpha)
        mods = _modulation(c, f_mod_w, f_mod_b, layer)
        act = _ffn_up(x, mods, f_w_up[layer], f_conv_w[layer], f_conv_b[layer])
        x = _out_proj_ln(x, mods, [act], f_w_down[layer], f_ln_g[layer], f_ln_b[layer], alpha)
    return x
```

```python
import functools

import jax
import jax.numpy as jnp
from jax import lax
from jax.experimental import pallas as pl
from jax.experimental.pallas import tpu as pltpu

F32 = jnp.float32
BF16 = jnp.bfloat16
I32 = jnp.int32

LANES = 128
SUBLANES = 8
V7X_VMEM_BYTES = 64 * 1024 * 1024
VMEM_LIMIT = V7X_VMEM_BYTES * 7 // 8

ROW_TILE = 512
MOD_COL_TILE = 512
INV_BLOCK = 16

CHUNK = 64
POOL_WINDOWS = (2, 4, 8, 16)
POOL_HALO = 16
DSA_HEADS = 8
DSA_HEAD_DIM = 64
IDX_HEADS = 4
IDX_DIM = 64
TOPK_MAX = 256
GDN_HEADS = 8
GDN_GROUP = 4
GDN_PACK = 4
SHORT_CONV = 4
FFN_CONV = 3
LN_EPS = 1e-5
RMS_EPS = 1e-6
NEG_BIG = -1e30
INT_MIN = -(2 ** 31)


def _dot(a, b):
    return jnp.dot(a, b, preferred_element_type=F32)


def _dot_nt(a, b):
    return lax.dot_general(a, b, (((1,), (1,)), ((), ())), preferred_element_type=F32)


def _split(a):
    hi = a.astype(BF16)
    lo = (a - hi.astype(F32)).astype(BF16)
    return hi, lo


def _sigmoid(x):
    return 1.0 / (1.0 + jnp.exp(-x))


def _silu(x):
    return x * _sigmoid(x)


def _layer_norm(z, g, b):
    mu = jnp.mean(z, axis=-1, keepdims=True)
    zc = z - mu
    var = jnp.mean(zc * zc, axis=-1, keepdims=True)
    return zc * lax.rsqrt(var + LN_EPS) * g + b


def _params(*sem):
    return pltpu.CompilerParams(dimension_semantics=sem, vmem_limit_bytes=VMEM_LIMIT)


def _mod_kernel(c_ref, w_ref, b_ref, o_ref):
    c = c_ref[...]
    a_hi, a_lo = _split(_silu(c))
    w_hi, w_lo = _split(w_ref[...])
    o_ref[...] = _dot(a_hi, w_hi) + _dot(a_hi, w_lo) + _dot(a_lo, w_hi) + b_ref[...]


def _modulation(c, mod_w, mod_b, layer):
    b, d = c.shape
    n3 = mod_w.shape[-1]
    tn = MOD_COL_TILE
    bias = mod_b.reshape(mod_b.shape[0], 1, n3)
    out = pl.pallas_call(
        _mod_kernel,
        grid=(n3 // tn,),
        in_specs=[pl.BlockSpec((b, d), lambda j: (0, 0)),
                  pl.BlockSpec((None, d, tn), lambda j: (layer, 0, j)),
                  pl.BlockSpec((None, 1, tn), lambda j: (layer, 0, j))],
        out_specs=pl.BlockSpec((b, tn), lambda j: (0, j)),
        out_shape=jax.ShapeDtypeStruct((b, n3), F32),
        compiler_params=_params("arbitrary"),
    )(c, mod_w, bias)
    return out.reshape(b, 3, d)


def _modulate(x, mod):
    return x * (1.0 + mod[1:2, :]) + mod[0:1, :]


def _even_in_kernel(x_ref, mod_ref, wm_ref, wih_ref, wil_ref, kvn_ref, wuk_ref,
                    u_ref, ql_ref, ckv_ref, qi_ref, kw_ref, *, pool_w, dsa_w, kv_rank):
    h = _modulate(x_ref[...], mod_ref[...])
    h_hi, h_lo = _split(h)
    main = _dot(h_hi, wm_ref[...])
    u_ref[...] = main[:, :pool_w]
    q = main[:, pool_w:pool_w + dsa_w]
    ql_ref[...] = _dot(q.astype(BF16), wuk_ref[...]).astype(BF16)
    ckv = main[:, pool_w + dsa_w:]
    ms = jnp.mean(ckv * ckv, axis=-1, keepdims=True)
    ckv_ref[...] = (ckv * lax.rsqrt(ms + RMS_EPS) * kvn_ref[...]).astype(BF16)
    idx = _dot(h_hi, wih_ref[...]) + _dot(h_hi, wil_ref[...]) + _dot(h_lo, wih_ref[...])
    nqi = IDX_HEADS * IDX_DIM
    qi_ref[...] = idx[:, :nqi]
    kw_ref[...] = idx[:, nqi:]


def _even_in_proj(x, mods, w_in, kv_norm, w_uk):
    b, s, d = x.shape
    pool_w = len(POOL_WINDOWS) * LANES
    dsa_w = DSA_HEADS * DSA_HEAD_DIM
    kv_rank = w_uk.shape[1]
    n_main = pool_w + dsa_w + kv_rank
    nqi = IDX_HEADS * IDX_DIM
    n_idx = nqi + LANES
    w_main = w_in[:, :n_main].astype(BF16)
    w_idx = jnp.pad(w_in[:, n_main:], ((0, 0), (0, n_main + n_idx - w_in.shape[1])))
    wih, wil = _split(w_idx)
    wuk_t = jnp.swapaxes(w_uk, 1, 2) * (DSA_HEAD_DIM ** -0.5)
    eye = jnp.eye(DSA_HEADS, dtype=F32)
    wuk_bd = (eye[:, None, :, None] * wuk_t[:, :, None, :]).reshape(dsa_w, DSA_HEADS * kv_rank).astype(BF16)
    tm = min(ROW_TILE, s)
    row = lambda c: pl.BlockSpec((None, tm, c), lambda i, j: (i, j, 0))
    full = lambda a: pl.BlockSpec(a.shape, lambda i, j: (0,) * a.ndim)
    kvn = kv_norm.reshape(1, kv_rank)
    kern = functools.partial(_even_in_kernel, pool_w=pool_w, dsa_w=dsa_w, kv_rank=kv_rank)
    return pl.pallas_call(
        kern,
        grid=(b, s // tm),
        in_specs=[row(d), pl.BlockSpec((None, 3, d), lambda i, j: (i, 0, 0)),
                  full(w_main), full(wih), full(wil), full(kvn), full(wuk_bd)],
        out_specs=[row(pool_w), row(DSA_HEADS * kv_rank), row(kv_rank), row(nqi), row(LANES)],
        out_shape=[jax.ShapeDtypeStruct((b, s, pool_w), F32),
                   jax.ShapeDtypeStruct((b, s, DSA_HEADS * kv_rank), BF16),
                   jax.ShapeDtypeStruct((b, s, kv_rank), BF16),
                   jax.ShapeDtypeStruct((b, s, nqi), F32),
                   jax.ShapeDtypeStruct((b, s, LANES), F32)],
        compiler_params=_params("arbitrary", "arbitrary"),
    )(x, mods, w_main, wih, wil, kvn, wuk_bd)


def _pool_kernel(u_ref, halo_ref, w_ref, scale_ref, o_ref, ext_ref, *, ts):
    i = pl.program_id(1)
    ext_ref[0:POOL_HALO, :] = jnp.where(i > 0, halo_ref[...], 0.0)
    ext_ref[POOL_HALO:POOL_HALO + ts, :] = u_ref[...]
    t = i * ts + lax.broadcasted_iota(I32, (ts, 1), 0)
    for g, win in enumerate(POOL_WINDOWS):
        cols = slice(g * LANES, (g + 1) * LANES)
        cur = ext_ref[POOL_HALO:POOL_HALO + ts, cols]
        acc = cur
        for j in range(1, win):
            acc = acc + ext_ref[POOL_HALO - j:POOL_HALO - j + ts, cols]
        cnt = jnp.minimum(t + 1, win).astype(F32)
        pooled = acc / cnt - cur
        y = _dot(pooled.astype(BF16), w_ref[g]) * scale_ref[:, cols]
        o_ref[:, cols] = y.astype(BF16)


def _pool_mixer(u, pool_w, pool_scale):
    b, s, width = u.shape
    ts = min(ROW_TILE, s)
    kern = functools.partial(_pool_kernel, ts=ts)
    hb = ts // POOL_HALO
    return pl.pallas_call(
        kern,
        grid=(b, s // ts),
        in_specs=[pl.BlockSpec((None, ts, width), lambda i, j: (i, j, 0)),
                  pl.BlockSpec((None, POOL_HALO, width), lambda i, j: (i, jnp.maximum(j * hb - 1, 0), 0)),
                  pl.BlockSpec(pool_w.shape, lambda i, j: (0, 0, 0)),
                  pl.BlockSpec((1, width), lambda i, j: (0, 0))],
        out_specs=pl.BlockSpec((None, ts, width), lambda i, j: (i, j, 0)),
        out_shape=jax.ShapeDtypeStruct((b, s, width), BF16),
        scratch_shapes=[pltpu.VMEM((ts + POOL_HALO, width), F32)],
        compiler_params=_params("arbitrary", "arbitrary"),
    )(u, u, pool_w.astype(BF16), pool_scale.reshape(1, width))


LOG2E = 1.4426950408889634
DSA_QUERY_BLOCK = 512
DSA_KEY_BLOCK = 512
ONES_ROWS = 16


def _dsat_kernel(ql_ref, qi_ref, kwq_ref, kw_ref, ckv_ref, tri_ref, wuv_ref, o_ref,
                 ki3_ref, kvt_ref, keys_ref, qlt_ref, m_ref, acc_ref, *, t, sk, topk, rank):
    qb = pl.program_id(1)
    lane = lax.broadcasted_iota(I32, (1, LANES), 1)
    low = lane < IDX_DIM
    nacc = 4 * SUBLANES

    @pl.when(qb == 0)
    def _():
        k_hi, k_lo = _split(jnp.where(low, kw_ref[...], 0.0))
        hi_f = k_hi.astype(F32)
        ki3_ref[:, 0:LANES] = (hi_f + pltpu.roll(hi_f, IDX_DIM, 1)).astype(BF16)
        ki3_ref[:, LANES:2 * LANES] = k_lo
        kvt_ref[0:rank, :] = ckv_ref[...].astype(F32).T.astype(BF16)
        kvt_ref[rank:rank + ONES_ROWS, :] = jnp.ones((ONES_ROWS, kvt_ref.shape[1]), BF16)

    parts = []
    for h in range(IDX_HEADS):
        xq = qi_ref[:, (h // 2) * LANES:(h // 2 + 1) * LANES]
        if h % 2:
            xq = pltpu.roll(xq, IDX_DIM, 1)
        q_hi, q_lo = _split(jnp.where(low, xq, 0.0))
        hi_f = q_hi.astype(F32)
        a = (hi_f + pltpu.roll(q_lo.astype(F32), IDX_DIM, 1)).astype(BF16)
        parts.append(jnp.concatenate([a, q_hi], axis=1))
    qi3 = jnp.concatenate(parts, axis=0)

    kwq_t = kwq_ref[...].T
    wscale = IDX_HEADS ** -0.5 * IDX_DIM ** -0.5
    wi = [kwq_t[IDX_DIM + h:IDX_DIM + h + 1, :] * wscale for h in range(IDX_HEADS)]

    for h in range(DSA_HEADS):
        qh = ql_ref[:, h * rank:(h + 1) * rank].astype(F32).T * LOG2E
        qlt_ref[:, h * t:(h + 1) * t] = qh.astype(BF16)

    qcol = lax.broadcasted_iota(I32, (1, t), 1)
    adm = qb * t + (qcol // CHUNK + 1) * CHUNK
    nkb = (qb * t + t + sk - 1) // sk
    krow = lax.broadcasted_iota(I32, (sk, 1), 0)
    i0 = jnp.int32(0)

    def score_body(j, carry):
        off = pl.multiple_of(j * sk, sk)
        sc = _dot_nt(ki3_ref[pl.ds(off, sk), :], qi3)
        score = jnp.zeros((sk, t), F32)
        for h in range(IDX_HEADS):
            score = score + wi[h] * jnp.maximum(sc[:, h * t:(h + 1) * t], 0.0)
        score = jnp.where(score == 0.0, 0.0, score)
        bits = lax.bitcast_convert_type(score, I32)
        key = bits ^ (lax.shift_right_arithmetic(bits, jnp.int32(31)) & jnp.int32(0x7FFFFFFF))
        keys_ref[pl.ds(off, sk), :] = jnp.where(krow + off < adm, key, jnp.int32(INT_MIN))
        return carry

    lax.fori_loop(i0, nkb, score_body, 0)

    def count(pred, ref):
        def body(j, acc):
            off = pl.multiple_of(j * sk, sk)
            cols = []
            for c in range(t // LANES):
                cs = slice(c * LANES, (c + 1) * LANES)
                hit = pred(keys_ref[pl.ds(off, sk), cs], ref[:, cs]).astype(F32)
                cols.append(jnp.sum(hit.reshape(sk // nacc, nacc, LANES), axis=0))
            return acc + jnp.concatenate(cols, axis=1)
        acc = lax.fori_loop(i0, nkb, body, jnp.zeros((nacc, t), F32))
        return jnp.sum(acc, axis=0, keepdims=True)

    ge = lambda k, r: k >= r
    kf = float(topk)
    zero = jnp.zeros((1, t), I32)
    res0 = jnp.where(count(ge, zero) >= kf, zero, jnp.int32(INT_MIN))

    def bit_body(i, res):
        cand = res | lax.shift_left(jnp.int32(1), jnp.int32(30) - i)
        return jnp.where(count(ge, cand) >= kf, cand, res)

    kth = lax.fori_loop(i0, jnp.int32(31), bit_body, res0)
    need = kf - count(lambda k, r: k > r, kth)

    m_ref[...] = jnp.full(m_ref.shape, NEG_BIG, F32)
    acc_ref[...] = jnp.zeros(acc_ref.shape, F32)

    def attn_body(j, seen, ties):
        off = pl.multiple_of(j * sk, sk)
        key = keys_ref[pl.ds(off, sk), :]
        if ties:
            eq = key == kth
            rank_eq = seen + _dot(tri_ref[...], eq.astype(F32).astype(BF16))
            sel = ((key > kth) | (eq & (rank_eq <= need))) & (krow + off < adm)
        else:
            rank_eq = jnp.broadcast_to(seen, (sk, t))
            sel = (key >= kth) & (krow + off < adm)
        bias = jnp.where(sel, jnp.float32(0.0), jnp.float32(NEG_BIG))
        logits = _dot(ckv_ref[pl.ds(off, sk), :], qlt_ref[...])
        kvt = kvt_ref[:, pl.ds(off, sk)]
        for h in range(DSA_HEADS):
            lg = logits[:, h * t:(h + 1) * t] + bias
            m_old = m_ref[h:h + 1, :]
            m_new = jnp.maximum(m_old, jnp.max(lg, axis=0, keepdims=True))
            alpha = jnp.exp2(m_old - m_new)
            p = jnp.exp2(lg - m_new)
            acc_ref[h] = alpha * acc_ref[h] + _dot(kvt, p.astype(BF16))
            m_ref[h:h + 1, :] = m_new
        return rank_eq[sk - 1:sk, :]

    seen0 = jnp.zeros((1, t), F32)
    excess = jnp.max(count(lambda k, r: k == r, kth) - need) > 0.0

    @pl.when(excess)
    def _():
        lax.fori_loop(i0, nkb, functools.partial(attn_body, ties=True), seen0)

    @pl.when(jnp.logical_not(excess))
    def _():
        lax.fori_loop(i0, nkb, functools.partial(attn_body, ties=False), seen0)

    y_t = []
    for h in range(DSA_HEADS):
        o_t = acc_ref[h, 0:rank, :] / acc_ref[h, rank:rank + 1, :]
        y_t.append(_dot(wuv_ref[h], o_t.astype(BF16)))
    o_ref[...] = jnp.concatenate(y_t, axis=0).T.astype(BF16)


def _dsa_attention_t(ql, qi, kw, ckv, w_uv):
    b, s, _ = ql.shape
    rank = ckv.shape[-1]
    t = min(DSA_QUERY_BLOCK, s)
    sk = min(DSA_KEY_BLOCK, s)
    topk = min(TOPK_MAX, s // 4)
    width = DSA_HEADS * DSA_HEAD_DIM
    tri = (lax.broadcasted_iota(I32, (sk, sk), 0) >= lax.broadcasted_iota(I32, (sk, sk), 1)).astype(BF16)
    wuv = jnp.swapaxes(w_uv, 1, 2).astype(BF16)
    kern = functools.partial(_dsat_kernel, t=t, sk=sk, topk=topk, rank=rank)
    blk = lambda c: pl.BlockSpec((None, t, c), lambda i, j: (i, j, 0))
    seq = lambda c: pl.BlockSpec((None, s, c), lambda i, j: (i, 0, 0))
    return pl.pallas_call(
        kern,
        grid=(b, s // t),
        in_specs=[blk(DSA_HEADS * rank), blk(qi.shape[-1]), blk(LANES), seq(LANES), seq(rank),
                  pl.BlockSpec((sk, sk), lambda i, j: (0, 0)),
                  pl.BlockSpec(wuv.shape, lambda i, j: (0, 0, 0))],
        out_specs=blk(width),
        out_shape=jax.ShapeDtypeStruct((b, s, width), BF16),
        scratch_shapes=[pltpu.VMEM((s, 2 * LANES), BF16),
                        pltpu.VMEM((rank + ONES_ROWS, s), BF16),
                        pltpu.VMEM((s, t), I32),
                        pltpu.VMEM((rank, DSA_HEADS * t), BF16),
                        pltpu.VMEM((DSA_HEADS, t), F32),
                        pltpu.VMEM((DSA_HEADS, rank + ONES_ROWS, t), F32)],
        compiler_params=_params("arbitrary", "arbitrary"),
    )(ql, qi, kw, kw, ckv, tri, wuv)


def _resln_kernel(*refs, n_y, widths, alpha):
    x_ref, mod_ref = refs[0], refs[1]
    y_refs = refs[2:2 + n_y]
    w_ref, g_ref, b_ref, o_ref = refs[2 + n_y:]
    acc = None
    off = 0
    for y_ref, wd in zip(y_refs, widths):
        part = _dot(y_ref[...], w_ref[off:off + wd, :])
        acc = part if acc is None else acc + part
        off += wd
    z = alpha * x_ref[...] + mod_ref[2:3, :] * acc
    o_ref[...] = _layer_norm(z, g_ref[...], b_ref[...])


def _out_proj_ln(x, mods, ys, w, ln_g, ln_b, alpha):
    b, s, d = x.shape
    widths = tuple(y.shape[-1] for y in ys)
    tm = min(ROW_TILE, s)
    kern = functools.partial(_resln_kernel, n_y=len(ys), widths=widths, alpha=alpha)
    row = lambda c: pl.BlockSpec((None, tm, c), lambda i, j: (i, j, 0))
    vec = pl.BlockSpec((1, d), lambda i, j: (0, 0))
    return pl.pallas_call(
        kern,
        grid=(b, s // tm),
        in_specs=[row(d), pl.BlockSpec((None, 3, d), lambda i, j: (i, 0, 0))]
                 + [row(wd) for wd in widths]
                 + [pl.BlockSpec(w.shape, lambda i, j: (0, 0)), vec, vec],
        out_specs=row(d),
        out_shape=jax.ShapeDtypeStruct((b, s, d), F32),
        compiler_params=_params("arbitrary", "arbitrary"),
    )(x, mods, *ys, w.astype(BF16), ln_g.reshape(1, d), ln_b.reshape(1, d))


def _ffn_up_kernel(x_ref, halo_ref, mod_ref, wa_ref, wv_ref, cwa_ref, cwv_ref, cba_ref, cbv_ref,
                   o_ref, h_ref, up_ref, *, tm):
    i = pl.program_id(1)
    j = pl.program_id(2)

    @pl.when(j == 0)
    def _():
        mod = mod_ref[...]
        halo = jnp.where(i > 0, _modulate(halo_ref[...], mod), 0.0)
        h_ref[0:SUBLANES, :] = halo.astype(BF16)
        h_ref[SUBLANES:SUBLANES + tm, :] = _modulate(x_ref[...], mod).astype(BF16)

    def conv(w_ref, cw_ref, cb_ref):
        up_ref[...] = _dot(h_ref[...], w_ref[...])
        out = cb_ref[...]
        for k in range(FFN_CONV):
            lo = SUBLANES - (FFN_CONV - 1) + k
            out = out + cw_ref[k:k + 1, :] * up_ref[lo:lo + tm, :]
        return out

    a = conv(wa_ref, cwa_ref, cba_ref)
    v = conv(wv_ref, cwv_ref, cbv_ref)
    o_ref[...] = (_silu(a) * v).astype(BF16)


def _ffn_up(x, mods, w_up, conv_w, conv_b):
    b, s, d = x.shape
    d_ff = w_up.shape[1] // 2
    tm = min(ROW_TILE, s)
    tn = d_ff
    ncol = d_ff // tn
    wb = w_up.astype(BF16)
    cb = conv_b.reshape(1, 2 * d_ff)
    hb = tm // SUBLANES
    kern = functools.partial(_ffn_up_kernel, tm=tm)
    col_a = lambda r: pl.BlockSpec((r, tn), lambda i, j, k: (0, k))
    col_v = lambda r: pl.BlockSpec((r, tn), lambda i, j, k: (0, k + ncol))
    return pl.pallas_call(
        kern,
        grid=(b, s // tm, ncol),
        in_specs=[pl.BlockSpec((None, tm, d), lambda i, j, k: (i, j, 0)),
                  pl.BlockSpec((None, SUBLANES, d), lambda i, j, k: (i, jnp.maximum(j * hb - 1, 0), 0)),
                  pl.BlockSpec((None, 3, d), lambda i, j, k: (i, 0, 0)),
                  col_a(d), col_v(d), col_a(FFN_CONV), col_v(FFN_CONV), col_a(1), col_v(1)],
        out_specs=pl.BlockSpec((None, tm, tn), lambda i, j, k: (i, j, k)),
        out_shape=jax.ShapeDtypeStruct((b, s, d_ff), BF16),
        scratch_shapes=[pltpu.VMEM((tm + SUBLANES, d), BF16),
                        pltpu.VMEM((tm + SUBLANES, tn), F32)],
        compiler_params=_params("arbitrary", "arbitrary", "arbitrary"),
    )(x, x, mods, wb, wb, conv_w, conv_w, cb, cb)


def _odd_in_kernel(x_ref, mod_ref, wq_ref, wg_ref, wbh_ref, wbl_ref, cw_ref, act_ref, gate_ref, ba_ref,
                   ext_ref, *, tm, width, dh):
    j = pl.program_id(1)
    nh = width // dh
    h = _modulate(x_ref[...], mod_ref[...])
    h_hi, h_lo = _split(h)

    @pl.when(j == 0)
    def _():
        ext_ref[0:SUBLANES, :] = jnp.zeros((SUBLANES, 3 * width), F32)

    @pl.when(j > 0)
    def _():
        ext_ref[0:SUBLANES, :] = ext_ref[tm:tm + SUBLANES, :]

    ext_ref[SUBLANES:SUBLANES + tm, :] = _dot(h_hi, wq_ref[...])
    for cb in range(3 * nh):
        cols = slice(cb * dh, (cb + 1) * dh)
        acc = None
        for k in range(SHORT_CONV):
            lo = SUBLANES - (SHORT_CONV - 1) + k
            term = cw_ref[k:k + 1, cols] * ext_ref[lo:lo + tm, cols]
            acc = term if acc is None else acc + term
        z = _silu(acc)
        if cb < 2 * nh:
            z = z * lax.rsqrt(jnp.sum(z * z, axis=-1, keepdims=True) + RMS_EPS)
            if cb < nh:
                z = z * (dh ** -0.5)
        act_ref[:, cols] = z

    gate_ref[...] = _dot(h_hi, wg_ref[...])
    ba_ref[...] = _dot(h_hi, wbh_ref[...]) + _dot(h_hi, wbl_ref[...]) + _dot(h_lo, wbh_ref[...])


def _odd_in_proj(x, mods, w_in, conv_w, width):
    b, s, d = x.shape
    dh = width // GDN_HEADS
    wq = w_in[:, :3 * width].astype(BF16)
    wg = w_in[:, 3 * width:4 * width].astype(BF16)
    wba = jnp.pad(w_in[:, 4 * width:], ((0, 0), (0, LANES - 2 * GDN_HEADS)))
    wbh, wbl = _split(wba)
    tm = min(ROW_TILE, s)
    row = lambda c: pl.BlockSpec((None, tm, c), lambda i, j: (i, j, 0))
    full = lambda a: pl.BlockSpec(a.shape, lambda i, j: (0, 0))
    kern = functools.partial(_odd_in_kernel, tm=tm, width=width, dh=dh)
    return pl.pallas_call(
        kern,
        grid=(b, s // tm),
        in_specs=[row(d), pl.BlockSpec((None, 3, d), lambda i, j: (i, 0, 0)),
                  full(wq), full(wg), full(wbh), full(wbl), full(conv_w)],
        out_specs=[row(3 * width), row(width), row(LANES)],
        out_shape=[jax.ShapeDtypeStruct((b, s, 3 * width), F32),
                   jax.ShapeDtypeStruct((b, s, width), F32),
                   jax.ShapeDtypeStruct((b, s, LANES), F32)],
        scratch_shapes=[pltpu.VMEM((tm + SUBLANES, 3 * width), F32)],
        compiler_params=_params("arbitrary", "arbitrary"),
    )(x, mods, wq, wg, wbh, wbl, conv_w)


def _bmm(a, b):
    return lax.dot_general(a, b, (((2,), (1,)), ((0,), (0,))), preferred_element_type=F32)


def _bmm_nt(a, b):
    return lax.dot_general(a, b, (((2,), (2,)), ((0,), (0,))), preferred_element_type=F32)


def _unit_lower_inverse_minus_eye(a, ii, jj, mm):
    diag = (ii // INV_BLOCK) == (jj // INV_BLOCK)
    ad = jnp.where(diag, a, 0.0)
    e = a - ad
    a2 = mm(ad, ad)
    a4 = mm(a2, a2)
    a8 = mm(a4, a4)
    p = -ad
    for sq in (a2, a4, a8):
        p = p + sq + mm(p, sq)
    m = e + mm(p, e)
    m2 = mm(m, m)
    q = m2 - m - mm(m, m2)
    return q + p + mm(q, p)


def _gdn_kernel(act_ref, gate_ref, ba_ref, alog_ref, dtb_ref, onorm_ref, ltri_ref, y_ref,
                state_ref, u_ref, w_ref, attn_ref, qg_ref, kdt_ref, el_ref, *, ts, width, dh):
    i = pl.program_id(1)
    nh = width // dh

    @pl.when(i == 0)
    def _():
        state_ref[...] = jnp.zeros(state_ref.shape, F32)

    pc = GDN_PACK * CHUNK
    ii = lax.broadcasted_iota(I32, (CHUNK, pc), 0)
    lane_c = lax.broadcasted_iota(I32, (CHUNK, pc), 1)
    jj = lane_c - (lane_c // CHUNK) * CHUNK
    blk_row = lax.broadcasted_iota(I32, (pc, 1), 0) // CHUNK
    mask_w = blk_row == lax.broadcasted_iota(I32, (1, GDN_PACK * dh), 1) // dh
    mask_c = blk_row == lax.broadcasted_iota(I32, (1, pc), 1) // CHUNK
    ltri = ltri_ref[...]

    def prep_body(cp, carry):
        r0 = pl.multiple_of(cp * (GDN_GROUP * CHUNK), GDN_GROUP * CHUNK)
        ba = ba_ref[pl.ds(r0, GDN_GROUP * CHUNK), :]
        beta = _sigmoid(ba)
        xg = ba + dtb_ref[...]
        softplus = jnp.maximum(xg, 0.0) + jnp.log1p(jnp.exp(-jnp.abs(xg)))
        g = -jnp.exp(alog_ref[...]) * softplus
        g1 = g.astype(BF16)
        r1 = g - g1.astype(F32)
        g2 = r1.astype(BF16)
        g3 = (r1 - g2.astype(F32)).astype(BF16)
        gc = _dot(ltri, g1) + _dot(ltri, g2) + _dot(ltri, g3)
        gc_t = gc.T
        egc = jnp.exp(gc)

        pairs = [(c, h) for c in range(GDN_GROUP) for h in range(nh)]
        rows_of = lambda c: slice(c * CHUNK, (c + 1) * CHUNK)
        col = lambda arr, lane: jnp.stack([arr[rows_of(c), lane(h):lane(h) + 1] for c, h in pairs])
        beta_t = beta.T
        egc_c = col(egc, lambda h: nh + h)
        act = lambda part: jnp.stack([act_ref[pl.ds(r0 + c * CHUNK, CHUNK),
                                              part * width + h * dh:part * width + (h + 1) * dh] for c, h in pairs])
        q, k, v = act(0), act(1), act(2)

        groups = [(c, hg) for c in range(GDN_GROUP) for hg in range(nh // GDN_PACK)]
        heads_of = lambda c, hg: [c * nh + hg * GDN_PACK + j for j in range(GDN_PACK)]
        pack = lambda x: jnp.stack([jnp.concatenate([x[n] for n in heads_of(c, hg)], axis=1) for c, hg in groups])
        bd = lambda x, mask: jnp.where(mask, jnp.concatenate([x] * GDN_PACK, axis=1), 0.0).astype(BF16)
        col_p = lambda arr, lane0: jnp.stack([jnp.concatenate(
            [jnp.broadcast_to(arr[rows_of(c), lane0 + hg * GDN_PACK + j:lane0 + hg * GDN_PACK + j + 1], (CHUNK, CHUNK))
             for j in range(GDN_PACK)], axis=1) for c, hg in groups])
        row_p = lambda arr_t, lane0: jnp.stack([jnp.concatenate(
            [arr_t[lane0 + hg * GDN_PACK + j:lane0 + hg * GDN_PACK + j + 1, rows_of(c)]
             for j in range(GDN_PACK)], axis=1) for c, hg in groups])
        gc_rp = row_p(gc_t, nh)
        dec = jnp.exp(jnp.where(ii >= jj, col_p(gc, nh) - gc_rp, 0.0))
        k_bd = bd(pack(k), mask_w)
        k_p = pack(k).astype(BF16)
        a = _bmm_nt(k_p, k_bd) * (col_p(beta, 0) * jnp.where(ii > jj, dec, 0.0))
        attn = _bmm_nt(pack(q).astype(BF16), k_bd) * jnp.where(ii >= jj, dec, 0.0)
        mm = lambda x, y: _bmm(x.astype(BF16), bd(y, mask_c))
        t_full = _unit_lower_inverse_minus_eye(a, ii, jj, mm) + jnp.where(ii == jj, 1.0, 0.0).astype(F32)
        t_beta = t_full * row_p(beta_t, 0)
        u_p = _bmm(t_beta.astype(BF16), bd(pack(v), mask_w))
        w_p = _bmm((t_beta * jnp.exp(gc_rp)).astype(BF16), k_bd)
        unpack = lambda x, wd: jnp.stack([x[m][:, j * wd:(j + 1) * wd]
                                          for m in range(len(groups)) for j in range(GDN_PACK)])
        n0 = pl.multiple_of(cp * (GDN_GROUP * nh), GDN_GROUP * nh)
        sl = pl.ds(n0, GDN_GROUP * nh)
        u_ref[sl] = unpack(u_p, dh)
        w_ref[sl] = unpack(w_p, dh).astype(BF16)
        attn_ref[sl] = unpack(attn, CHUNK).astype(BF16)
        qg_ref[sl] = (q * egc_c).astype(BF16)
        gc_r = jnp.stack([gc_t[nh + h:nh + h + 1, rows_of(c)] for c, h in pairs])
        g_last = gc_r[:, :, CHUNK - 1:CHUNK]
        kdt_ref[sl] = (jnp.swapaxes(k, 1, 2) * jnp.exp(g_last - gc_r)).astype(BF16)
        el_ref[sl] = jnp.broadcast_to(jnp.exp(g_last), (GDN_GROUP * nh, 1, dh))
        return carry

    lax.fori_loop(jnp.int32(0), jnp.int32(ts // (GDN_GROUP * CHUNK)), prep_body, 0)

    def chunk_body(c, carry):
        r0 = pl.multiple_of(c * CHUNK, CHUNK)
        sl = pl.ds(pl.multiple_of(c * nh, nh), nh)
        st = state_ref[...]
        st_bf = st.astype(BF16)
        v_new = u_ref[sl] - _bmm(w_ref[sl], st_bf)
        vn_bf = v_new.astype(BF16)
        o = _bmm(qg_ref[sl], st_bf) + _bmm(attn_ref[sl], vn_bf)
        state_ref[...] = st * el_ref[sl] + _bmm(kdt_ref[sl], vn_bf)
        on = o * lax.rsqrt(jnp.mean(o * o, axis=-1, keepdims=True) + RMS_EPS) * onorm_ref[...]
        for h in range(nh):
            hc = slice(h * dh, (h + 1) * dh)
            y_ref[pl.ds(r0, CHUNK), hc] = (on[h] * _silu(gate_ref[pl.ds(r0, CHUNK), hc])).astype(BF16)
        return carry

    lax.fori_loop(jnp.int32(0), jnp.int32(ts // CHUNK), chunk_body, 0)


def _gdn(act, gate, ba, a_log, dt_bias, out_norm):
    b, s, w3 = act.shape
    width = w3 // 3
    dh = width // GDN_HEADS
    ts = min(ROW_TILE, s)
    pad_a = lambda v: jnp.pad(v.reshape(1, GDN_HEADS), ((0, 0), (GDN_HEADS, LANES - 2 * GDN_HEADS)))
    gc_rows = GDN_GROUP * CHUNK
    ri = lax.broadcasted_iota(I32, (gc_rows, gc_rows), 0)
    ci = lax.broadcasted_iota(I32, (gc_rows, gc_rows), 1)
    ltri = ((ri >= ci) & (ri // CHUNK == ci // CHUNK)).astype(BF16)
    n_mat = (ts // CHUNK) * GDN_HEADS
    kern = functools.partial(_gdn_kernel, ts=ts, width=width, dh=dh)
    row = lambda c: pl.BlockSpec((None, ts, c), lambda i, j: (i, j, 0))
    full = lambda shape: pl.BlockSpec(shape, lambda i, j: (0,) * len(shape))
    return pl.pallas_call(
        kern,
        grid=(b, s // ts),
        in_specs=[row(w3), row(width), row(LANES), full((1, LANES)), full((1, LANES)),
                  full((1, dh)), full((gc_rows, gc_rows))],
        out_specs=row(width),
        out_shape=jax.ShapeDtypeStruct((b, s, width), BF16),
        scratch_shapes=[pltpu.VMEM((GDN_HEADS, dh, dh), F32),
                        pltpu.VMEM((n_mat, CHUNK, dh), F32),
                        pltpu.VMEM((n_mat, CHUNK, dh), BF16),
                        pltpu.VMEM((n_mat, CHUNK, CHUNK), BF16),
                        pltpu.VMEM((n_mat, CHUNK, dh), BF16),
                        pltpu.VMEM((n_mat, dh, CHUNK), BF16),
                        pltpu.VMEM((n_mat, 1, dh), F32)],
        compiler_params=_params("arbitrary", "arbitrary"),
    )(act, gate, ba, pad_a(a_log), pad_a(dt_bias), out_norm.reshape(1, dh), ltri)


def kernel(x, c, e_mod_w, e_mod_b, e_w_in, e_pool_w, e_pool_scale, e_kv_norm, e_w_uk, e_w_uv, e_w_out, e_ln_g, e_ln_b, o_mod_w, o_mod_b, o_w_in, o_conv_w, o_a_log, o_dt_bias, o_out_norm, o_w_out, o_ln_g, o_ln_b, f_mod_w, f_mod_b, f_w_up, f_conv_w, f_conv_b, f_w_down, f_ln_g, f_ln_b):
    depth = f_mod_w.shape[0]
    alpha = (2 * depth) ** 0.25
    for layer in range(depth):
        i = layer // 2
        if layer % 2 == 0:
            mods = _modulation(c, e_mod_w, e_mod_b, i)
            u, ql, ckv, qi, kw = _even_in_proj(x, mods, e_w_in[i], e_kv_norm[i], e_w_uk[i])
            y_pool = _pool_mixer(u, e_pool_w[i], e_pool_scale[i])
            y_dsa = _dsa_attention_t(ql, qi, kw, ckv, e_w_uv[i])
            x = _out_proj_ln(x, mods, [y_pool, y_dsa], e_w_out[i], e_ln_g[i], e_ln_b[i], alpha)
        else:
            mods = _modulation(c, o_mod_w, o_mod_b, i)
            width = o_w_out.shape[1]
            act, gate, ba = _odd_in_proj(x, mods, o_w_in[i], o_conv_w[i], width)
            y = _gdn(act, gate, ba, o_a_log[i], o_dt_bias[i], o_out_norm[i])
            x = _out_proj_ln(x, mods, [y], o_w_out[i], o_ln_g[i], o_ln_b[i], alpha)
        mods = _modulation(c, f_mod_w, f_mod_b, layer)
        act = _ffn_up(x, mods, f_w_up[layer], f_conv_w[layer], f_conv_b[layer])
        x = _out_proj_ln(x, mods, [act], f_w_down[layer], f_ln_g[layer], f_ln_b[layer], alpha)
    return x
```

```python
import functools

import jax
import jax.numpy as jnp
from jax import lax
from jax.experimental import pallas as pl
from jax.experimental.pallas import tpu as pltpu

F32 = jnp.float32
BF16 = jnp.bfloat16
I32 = jnp.int32

LANES = 128
SUBLANES = 8
V7X_VMEM_BYTES = 64 * 1024 * 1024
VMEM_LIMIT = V7X_VMEM_BYTES * 7 // 8

ROW_TILE = 512
MOD_COL_TILE = 512
INV_BLOCK = 16

CHUNK = 64
POOL_WINDOWS = (2, 4, 8, 16)
POOL_HALO = 16
DSA_HEADS = 8
DSA_HEAD_DIM = 64
IDX_HEADS = 4
IDX_DIM = 64
TOPK_MAX = 256
GDN_HEADS = 8
GDN_GROUP = 4
GDN_PACK = 4
GDN_SEQS = 2
SHORT_CONV = 4
FFN_CONV = 3
LN_EPS = 1e-5
RMS_EPS = 1e-6
NEG_BIG = -1e30
INT_MIN = -(2 ** 31)


def _dot(a, b):
    return jnp.dot(a, b, preferred_element_type=F32)


def _dot_nt(a, b):
    return lax.dot_general(a, b, (((1,), (1,)), ((), ())), preferred_element_type=F32)


def _split(a):
    hi = a.astype(BF16)
    lo = (a - hi.astype(F32)).astype(BF16)
    return hi, lo


def _sigmoid(x):
    return 1.0 / (1.0 + jnp.exp(-x))


def _silu(x):
    return x * _sigmoid(x)


def _layer_norm(z, g, b):
    mu = jnp.mean(z, axis=-1, keepdims=True)
    zc = z - mu
    var = jnp.mean(zc * zc, axis=-1, keepdims=True)
    return zc * lax.rsqrt(var + LN_EPS) * g + b


def _params(*sem):
    return pltpu.CompilerParams(dimension_semantics=sem, vmem_limit_bytes=VMEM_LIMIT)


def _mod_kernel(c_ref, w_ref, b_ref, o_ref):
    c = c_ref[...]
    a_hi, a_lo = _split(_silu(c))
    w_hi, w_lo = _split(w_ref[...])
    o_ref[...] = _dot(a_hi, w_hi) + _dot(a_hi, w_lo) + _dot(a_lo, w_hi) + b_ref[...]


def _modulation(c, mod_w, mod_b, layer):
    b, d = c.shape
    n3 = mod_w.shape[-1]
    tn = MOD_COL_TILE
    bias = mod_b.reshape(mod_b.shape[0], 1, n3)
    out = pl.pallas_call(
        _mod_kernel,
        grid=(n3 // tn,),
        in_specs=[pl.BlockSpec((b, d), lambda j: (0, 0)),
                  pl.BlockSpec((None, d, tn), lambda j: (layer, 0, j)),
                  pl.BlockSpec((None, 1, tn), lambda j: (layer, 0, j))],
        out_specs=pl.BlockSpec((b, tn), lambda j: (0, j)),
        out_shape=jax.ShapeDtypeStruct((b, n3), F32),
        compiler_params=_params("arbitrary"),
    )(c, mod_w, bias)
    return out.reshape(b, 3, d)


def _modulate(x, mod):
    return x * (1.0 + mod[1:2, :]) + mod[0:1, :]


def _even_in_kernel(x_ref, mod_ref, wm_ref, wih_ref, wil_ref, kvn_ref, wuk_ref,
                    u_ref, ql_ref, ckv_ref, qi_ref, kw_ref, *, pool_w, dsa_w, kv_rank):
    h = _modulate(x_ref[...], mod_ref[...])
    h_hi, h_lo = _split(h)
    main = _dot(h_hi, wm_ref[...])
    u_ref[...] = main[:, :pool_w]
    q = main[:, pool_w:pool_w + dsa_w]
    ql_ref[...] = _dot(q.astype(BF16), wuk_ref[...]).astype(BF16)
    ckv = main[:, pool_w + dsa_w:]
    ms = jnp.mean(ckv * ckv, axis=-1, keepdims=True)
    ckv_ref[...] = (ckv * lax.rsqrt(ms + RMS_EPS) * kvn_ref[...]).astype(BF16)
    idx = _dot(h_hi, wih_ref[...]) + _dot(h_hi, wil_ref[...]) + _dot(h_lo, wih_ref[...])
    nqi = IDX_HEADS * IDX_DIM
    qi_ref[...] = idx[:, :nqi]
    kw_ref[...] = idx[:, nqi:]


def _even_in_proj(x, mods, w_in, kv_norm, w_uk):
    b, s, d = x.shape
    pool_w = len(POOL_WINDOWS) * LANES
    dsa_w = DSA_HEADS * DSA_HEAD_DIM
    kv_rank = w_uk.shape[1]
    n_main = pool_w + dsa_w + kv_rank
    nqi = IDX_HEADS * IDX_DIM
    n_idx = nqi + LANES
    w_main = w_in[:, :n_main].astype(BF16)
    w_idx = jnp.pad(w_in[:, n_main:], ((0, 0), (0, n_main + n_idx - w_in.shape[1])))
    wih, wil = _split(w_idx)
    wuk_t = jnp.swapaxes(w_uk, 1, 2) * (DSA_HEAD_DIM ** -0.5)
    eye = jnp.eye(DSA_HEADS, dtype=F32)
    wuk_bd = (eye[:, None, :, None] * wuk_t[:, :, None, :]).reshape(dsa_w, DSA_HEADS * kv_rank).astype(BF16)
    tm = min(ROW_TILE, s)
    row = lambda c: pl.BlockSpec((None, tm, c), lambda i, j: (i, j, 0))
    full = lambda a: pl.BlockSpec(a.shape, lambda i, j: (0,) * a.ndim)
    kvn = kv_norm.reshape(1, kv_rank)
    kern = functools.partial(_even_in_kernel, pool_w=pool_w, dsa_w=dsa_w, kv_rank=kv_rank)
    return pl.pallas_call(
        kern,
        grid=(b, s // tm),
        in_specs=[row(d), pl.BlockSpec((None, 3, d), lambda i, j: (i, 0, 0)),
                  full(w_main), full(wih), full(wil), full(kvn), full(wuk_bd)],
        out_specs=[row(pool_w), row(DSA_HEADS * kv_rank), row(kv_rank), row(nqi), row(LANES)],
        out_shape=[jax.ShapeDtypeStruct((b, s, pool_w), F32),
                   jax.ShapeDtypeStruct((b, s, DSA_HEADS * kv_rank), BF16),
                   jax.ShapeDtypeStruct((b, s, kv_rank), BF16),
                   jax.ShapeDtypeStruct((b, s, nqi), F32),
                   jax.ShapeDtypeStruct((b, s, LANES), F32)],
        compiler_params=_params("arbitrary", "arbitrary"),
    )(x, mods, w_main, wih, wil, kvn, wuk_bd)


def _pool_kernel(u_ref, halo_ref, w_ref, scale_ref, o_ref, ext_ref, *, ts):
    i = pl.program_id(1)
    ext_ref[0:POOL_HALO, :] = jnp.where(i > 0, halo_ref[...], 0.0)
    ext_ref[POOL_HALO:POOL_HALO + ts, :] = u_ref[...]
    t = i * ts + lax.broadcasted_iota(I32, (ts, 1), 0)
    for g, win in enumerate(POOL_WINDOWS):
        cols = slice(g * LANES, (g + 1) * LANES)
        cur = ext_ref[POOL_HALO:POOL_HALO + ts, cols]
        acc = cur
        for j in range(1, win):
            acc = acc + ext_ref[POOL_HALO - j:POOL_HALO - j + ts, cols]
        cnt = jnp.minimum(t + 1, win).astype(F32)
        pooled = acc / cnt - cur
        y = _dot(pooled.astype(BF16), w_ref[g]) * scale_ref[:, cols]
        o_ref[:, cols] = y.astype(BF16)


def _pool_mixer(u, pool_w, pool_scale):
    b, s, width = u.shape
    ts = min(ROW_TILE, s)
    kern = functools.partial(_pool_kernel, ts=ts)
    hb = ts // POOL_HALO
    return pl.pallas_call(
        kern,
        grid=(b, s // ts),
        in_specs=[pl.BlockSpec((None, ts, width), lambda i, j: (i, j, 0)),
                  pl.BlockSpec((None, POOL_HALO, width), lambda i, j: (i, jnp.maximum(j * hb - 1, 0), 0)),
                  pl.BlockSpec(pool_w.shape, lambda i, j: (0, 0, 0)),
                  pl.BlockSpec((1, width), lambda i, j: (0, 0))],
        out_specs=pl.BlockSpec((None, ts, width), lambda i, j: (i, j, 0)),
        out_shape=jax.ShapeDtypeStruct((b, s, width), BF16),
        scratch_shapes=[pltpu.VMEM((ts + POOL_HALO, width), F32)],
        compiler_params=_params("arbitrary", "arbitrary"),
    )(u, u, pool_w.astype(BF16), pool_scale.reshape(1, width))


LOG2E = 1.4426950408889634
DSA_QUERY_BLOCK = 512
DSA_KEY_BLOCK = 512
ONES_ROWS = 16


def _dsat_kernel(ql_ref, qi_ref, kwq_ref, kw_ref, ckv_ref, tri_ref, wuv_ref, o_ref,
                 ki3_ref, kvt_ref, keys_ref, qlt_ref, m_ref, acc_ref, *, t, sk, topk, rank):
    qb = pl.program_id(1)
    lane = lax.broadcasted_iota(I32, (1, LANES), 1)
    low = lane < IDX_DIM
    nacc = 4 * SUBLANES

    @pl.when(qb == 0)
    def _():
        k_hi, k_lo = _split(jnp.where(low, kw_ref[...], 0.0))
        hi_f = k_hi.astype(F32)
        ki3_ref[:, 0:LANES] = (hi_f + pltpu.roll(hi_f, IDX_DIM, 1)).astype(BF16)
        ki3_ref[:, LANES:2 * LANES] = k_lo
        kvt_ref[0:rank, :] = ckv_ref[...].astype(F32).T.astype(BF16)
        kvt_ref[rank:rank + ONES_ROWS, :] = jnp.ones((ONES_ROWS, kvt_ref.shape[1]), BF16)

    parts = []
    for h in range(IDX_HEADS):
        xq = qi_ref[:, (h // 2) * LANES:(h // 2 + 1) * LANES]
        if h % 2:
            xq = pltpu.roll(xq, IDX_DIM, 1)
        q_hi, q_lo = _split(jnp.where(low, xq, 0.0))
        hi_f = q_hi.astype(F32)
        a = (hi_f + pltpu.roll(q_lo.astype(F32), IDX_DIM, 1)).astype(BF16)
        parts.append(jnp.concatenate([a, q_hi], axis=1))
    qi3 = jnp.concatenate(parts, axis=0)

    kwq_t = kwq_ref[...].T
    wscale = IDX_HEADS ** -0.5 * IDX_DIM ** -0.5
    wi = [kwq_t[IDX_DIM + h:IDX_DIM + h + 1, :] * wscale for h in range(IDX_HEADS)]

    for h in range(DSA_HEADS):
        qh = ql_ref[:, h * rank:(h + 1) * rank].astype(F32).T * LOG2E
        qlt_ref[:, h * t:(h + 1) * t] = qh.astype(BF16)

    qcol = lax.broadcasted_iota(I32, (1, t), 1)
    adm = qb * t + (qcol // CHUNK + 1) * CHUNK
    nkb = (qb * t + t + sk - 1) // sk
    krow = lax.broadcasted_iota(I32, (sk, 1), 0)
    i0 = jnp.int32(0)

    def score_body(j, carry):
        off = pl.multiple_of(j * sk, sk)
        sc = _dot_nt(ki3_ref[pl.ds(off, sk), :], qi3)
        score = jnp.zeros((sk, t), F32)
        for h in range(IDX_HEADS):
            score = score + wi[h] * jnp.maximum(sc[:, h * t:(h + 1) * t], 0.0)
        score = jnp.where(score == 0.0, 0.0, score)
        bits = lax.bitcast_convert_type(score, I32)
        key = bits ^ (lax.shift_right_arithmetic(bits, jnp.int32(31)) & jnp.int32(0x7FFFFFFF))
        keys_ref[pl.ds(off, sk), :] = jnp.where(krow + off < adm, key, jnp.int32(INT_MIN))
        return carry

    lax.fori_loop(i0, nkb, score_body, 0)

    def count(pred, ref):
        def body(j, acc):
            off = pl.multiple_of(j * sk, sk)
            cols = []
            for c in range(t // LANES):
                cs = slice(c * LANES, (c + 1) * LANES)
                hit = pred(keys_ref[pl.ds(off, sk), cs], ref[:, cs]).astype(F32)
                cols.append(jnp.sum(hit.reshape(sk // nacc, nacc, LANES), axis=0))
            return acc + jnp.concatenate(cols, axis=1)
        acc = lax.fori_loop(i0, nkb, body, jnp.zeros((nacc, t), F32))
        return jnp.sum(acc, axis=0, keepdims=True)

    ge = lambda k, r: k >= r
    kf = float(topk)
    zero = jnp.zeros((1, t), I32)
    res0 = jnp.where(count(ge, zero) >= kf, zero, jnp.int32(INT_MIN))

    def bit_body(i, res):
        cand = res | lax.shift_left(jnp.int32(1), jnp.int32(30) - i)
        return jnp.where(count(ge, cand) >= kf, cand, res)

    kth = lax.fori_loop(i0, jnp.int32(31), bit_body, res0)
    need = kf - count(lambda k, r: k > r, kth)

    m_ref[...] = jnp.full(m_ref.shape, NEG_BIG, F32)
    acc_ref[...] = jnp.zeros(acc_ref.shape, F32)

    def attn_body(j, seen, ties):
        off = pl.multiple_of(j * sk, sk)
        key = keys_ref[pl.ds(off, sk), :]
        if ties:
            eq = key == kth
            rank_eq = seen + _dot(tri_ref[...], eq.astype(F32).astype(BF16))
            sel = ((key > kth) | (eq & (rank_eq <= need))) & (krow + off < adm)
        else:
            rank_eq = jnp.broadcast_to(seen, (sk, t))
            sel = (key >= kth) & (krow + off < adm)
        bias = jnp.where(sel, jnp.float32(0.0), jnp.float32(NEG_BIG))
        logits = _dot(ckv_ref[pl.ds(off, sk), :], qlt_ref[...])
        kvt = kvt_ref[:, pl.ds(off, sk)]
        for h in range(DSA_HEADS):
            lg = logits[:, h * t:(h + 1) * t] + bias
            m_old = m_ref[h:h + 1, :]
            m_new = jnp.maximum(m_old, jnp.max(lg, axis=0, keepdims=True))
            alpha = jnp.exp2(m_old - m_new)
            p = jnp.exp2(lg - m_new)
            acc_ref[h] = alpha * acc_ref[h] + _dot(kvt, p.astype(BF16))
            m_ref[h:h + 1, :] = m_new
        return rank_eq[sk - 1:sk, :]

    seen0 = jnp.zeros((1, t), F32)
    excess = jnp.max(count(lambda k, r: k == r, kth) - need) > 0.0

    @pl.when(excess)
    def _():
        lax.fori_loop(i0, nkb, functools.partial(attn_body, ties=True), seen0)

    @pl.when(jnp.logical_not(excess))
    def _():
        lax.fori_loop(i0, nkb, functools.partial(attn_body, ties=False), seen0)

    y_t = []
    for h in range(DSA_HEADS):
        o_t = acc_ref[h, 0:rank, :] / acc_ref[h, rank:rank + 1, :]
        y_t.append(_dot(wuv_ref[h], o_t.astype(BF16)))
    o_ref[...] = jnp.concatenate(y_t, axis=0).T.astype(BF16)


def _dsa_attention_t(ql, qi, kw, ckv, w_uv):
    b, s, _ = ql.shape
    rank = ckv.shape[-1]
    t = min(DSA_QUERY_BLOCK, s)
    sk = min(DSA_KEY_BLOCK, s)
    topk = min(TOPK_MAX, s // 4)
    width = DSA_HEADS * DSA_HEAD_DIM
    tri = (lax.broadcasted_iota(I32, (sk, sk), 0) >= lax.broadcasted_iota(I32, (sk, sk), 1)).astype(BF16)
    wuv = jnp.swapaxes(w_uv, 1, 2).astype(BF16)
    kern = functools.partial(_dsat_kernel, t=t, sk=sk, topk=topk, rank=rank)
    blk = lambda c: pl.BlockSpec((None, t, c), lambda i, j: (i, j, 0))
    seq = lambda c: pl.BlockSpec((None, s, c), lambda i, j: (i, 0, 0))
    return pl.pallas_call(
        kern,
        grid=(b, s // t),
        in_specs=[blk(DSA_HEADS * rank), blk(qi.shape[-1]), blk(LANES), seq(LANES), seq(rank),
                  pl.BlockSpec((sk, sk), lambda i, j: (0, 0)),
                  pl.BlockSpec(wuv.shape, lambda i, j: (0, 0, 0))],
        out_specs=blk(width),
        out_shape=jax.ShapeDtypeStruct((b, s, width), BF16),
        scratch_shapes=[pltpu.VMEM((s, 2 * LANES), BF16),
                        pltpu.VMEM((rank + ONES_ROWS, s), BF16),
                        pltpu.VMEM((s, t), I32),
                        pltpu.VMEM((rank, DSA_HEADS * t), BF16),
                        pltpu.VMEM((DSA_HEADS, t), F32),
                        pltpu.VMEM((DSA_HEADS, rank + ONES_ROWS, t), F32)],
        compiler_params=_params("arbitrary", "arbitrary"),
    )(ql, qi, kw, kw, ckv, tri, wuv)


def _resln_kernel(*refs, n_y, widths, alpha):
    x_ref, mod_ref = refs[0], refs[1]
    y_refs = refs[2:2 + n_y]
    w_ref, g_ref, b_ref, o_ref = refs[2 + n_y:]
    acc = None
    off = 0
    for y_ref, wd in zip(y_refs, widths):
        part = _dot(y_ref[...], w_ref[off:off + wd, :])
        acc = part if acc is None else acc + part
        off += wd
    z = alpha * x_ref[...] + mod_ref[2:3, :] * acc
    o_ref[...] = _layer_norm(z, g_ref[...], b_ref[...])


def _out_proj_ln(x, mods, ys, w, ln_g, ln_b, alpha):
    b, s, d = x.shape
    widths = tuple(y.shape[-1] for y in ys)
    tm = min(ROW_TILE, s)
    kern = functools.partial(_resln_kernel, n_y=len(ys), widths=widths, alpha=alpha)
    row = lambda c: pl.BlockSpec((None, tm, c), lambda i, j: (i, j, 0))
    vec = pl.BlockSpec((1, d), lambda i, j: (0, 0))
    return pl.pallas_call(
        kern,
        grid=(b, s // tm),
        in_specs=[row(d), pl.BlockSpec((None, 3, d), lambda i, j: (i, 0, 0))]
                 + [row(wd) for wd in widths]
                 + [pl.BlockSpec(w.shape, lambda i, j: (0, 0)), vec, vec],
        out_specs=row(d),
        out_shape=jax.ShapeDtypeStruct((b, s, d), F32),
        compiler_params=_params("arbitrary", "arbitrary"),
    )(x, mods, *ys, w.astype(BF16), ln_g.reshape(1, d), ln_b.reshape(1, d))


def _ffn_up_kernel(x_ref, halo_ref, mod_ref, wa_ref, wv_ref, cwa_ref, cwv_ref, cba_ref, cbv_ref,
                   o_ref, h_ref, up_ref, *, tm):
    i = pl.program_id(1)
    j = pl.program_id(2)

    @pl.when(j == 0)
    def _():
        mod = mod_ref[...]
        halo = jnp.where(i > 0, _modulate(halo_ref[...], mod), 0.0)
        h_ref[0:SUBLANES, :] = halo.astype(BF16)
        h_ref[SUBLANES:SUBLANES + tm, :] = _modulate(x_ref[...], mod).astype(BF16)

    def conv(w_ref, cw_ref, cb_ref):
        up_ref[...] = _dot(h_ref[...], w_ref[...])
        out = cb_ref[...]
        for k in range(FFN_CONV):
            lo = SUBLANES - (FFN_CONV - 1) + k
            out = out + cw_ref[k:k + 1, :] * up_ref[lo:lo + tm, :]
        return out

    a = conv(wa_ref, cwa_ref, cba_ref)
    v = conv(wv_ref, cwv_ref, cbv_ref)
    o_ref[...] = (_silu(a) * v).astype(BF16)


def _ffn_up(x, mods, w_up, conv_w, conv_b):
    b, s, d = x.shape
    d_ff = w_up.shape[1] // 2
    tm = min(ROW_TILE, s)
    tn = d_ff
    ncol = d_ff // tn
    wb = w_up.astype(BF16)
    cb = conv_b.reshape(1, 2 * d_ff)
    hb = tm // SUBLANES
    kern = functools.partial(_ffn_up_kernel, tm=tm)
    col_a = lambda r: pl.BlockSpec((r, tn), lambda i, j, k: (0, k))
    col_v = lambda r: pl.BlockSpec((r, tn), lambda i, j, k: (0, k + ncol))
    return pl.pallas_call(
        kern,
        grid=(b, s // tm, ncol),
        in_specs=[pl.BlockSpec((None, tm, d), lambda i, j, k: (i, j, 0)),
                  pl.BlockSpec((None, SUBLANES, d), lambda i, j, k: (i, jnp.maximum(j * hb - 1, 0), 0)),
                  pl.BlockSpec((None, 3, d), lambda i, j, k: (i, 0, 0)),
                  col_a(d), col_v(d), col_a(FFN_CONV), col_v(FFN_CONV), col_a(1), col_v(1)],
        out_specs=pl.BlockSpec((None, tm, tn), lambda i, j, k: (i, j, k)),
        out_shape=jax.ShapeDtypeStruct((b, s, d_ff), BF16),
        scratch_shapes=[pltpu.VMEM((tm + SUBLANES, d), BF16),
                        pltpu.VMEM((tm + SUBLANES, tn), F32)],
        compiler_params=_params("arbitrary", "arbitrary", "arbitrary"),
    )(x, x, mods, wb, wb, conv_w, conv_w, cb, cb)


def _odd_in_kernel(x_ref, mod_ref, wq_ref, wg_ref, wbh_ref, wbl_ref, cw_ref, act_ref, gate_ref, ba_ref,
                   ext_ref, *, tm, width, dh):
    j = pl.program_id(1)
    nh = width // dh
    h = _modulate(x_ref[...], mod_ref[...])
    h_hi, h_lo = _split(h)

    @pl.when(j == 0)
    def _():
        ext_ref[0:SUBLANES, :] = jnp.zeros((SUBLANES, 3 * width), F32)

    @pl.when(j > 0)
    def _():
        ext_ref[0:SUBLANES, :] = ext_ref[tm:tm + SUBLANES, :]

    ext_ref[SUBLANES:SUBLANES + tm, :] = _dot(h_hi, wq_ref[...])
    for cb in range(3 * nh):
        cols = slice(cb * dh, (cb + 1) * dh)
        acc = None
        for k in range(SHORT_CONV):
            lo = SUBLANES - (SHORT_CONV - 1) + k
            term = cw_ref[k:k + 1, cols] * ext_ref[lo:lo + tm, cols]
            acc = term if acc is None else acc + term
        z = _silu(acc)
        if cb < 2 * nh:
            z = z * lax.rsqrt(jnp.sum(z * z, axis=-1, keepdims=True) + RMS_EPS)
            if cb < nh:
                z = z * (dh ** -0.5)
        act_ref[:, cols] = z

    gate_ref[...] = _dot(h_hi, wg_ref[...])
    ba_ref[...] = _dot(h_hi, wbh_ref[...]) + _dot(h_hi, wbl_ref[...]) + _dot(h_lo, wbh_ref[...])


def _odd_in_proj(x, mods, w_in, conv_w, width):
    b, s, d = x.shape
    dh = width // GDN_HEADS
    wq = w_in[:, :3 * width].astype(BF16)
    wg = w_in[:, 3 * width:4 * width].astype(BF16)
    wba = jnp.pad(w_in[:, 4 * width:], ((0, 0), (0, LANES - 2 * GDN_HEADS)))
    wbh, wbl = _split(wba)
    tm = min(ROW_TILE, s)
    row = lambda c: pl.BlockSpec((None, tm, c), lambda i, j: (i, j, 0))
    full = lambda a: pl.BlockSpec(a.shape, lambda i, j: (0, 0))
    kern = functools.partial(_odd_in_kernel, tm=tm, width=width, dh=dh)
    return pl.pallas_call(
        kern,
        grid=(b, s // tm),
        in_specs=[row(d), pl.BlockSpec((None, 3, d), lambda i, j: (i, 0, 0)),
                  full(wq), full(wg), full(wbh), full(wbl), full(conv_w)],
        out_specs=[row(3 * width), row(width), row(LANES)],
        out_shape=[jax.ShapeDtypeStruct((b, s, 3 * width), F32),
                   jax.ShapeDtypeStruct((b, s, width), F32),
                   jax.ShapeDtypeStruct((b, s, LANES), F32)],
        scratch_shapes=[pltpu.VMEM((tm + SUBLANES, 3 * width), F32)],
        compiler_params=_params("arbitrary", "arbitrary"),
    )(x, mods, wq, wg, wbh, wbl, conv_w)


def _bmm(a, b):
    return lax.dot_general(a, b, (((2,), (1,)), ((0,), (0,))), preferred_element_type=F32)


def _bmm_nt(a, b):
    return lax.dot_general(a, b, (((2,), (2,)), ((0,), (0,))), preferred_element_type=F32)


def _unit_lower_inverse_minus_eye(a, ii, jj, mm):
    diag = (ii // INV_BLOCK) == (jj // INV_BLOCK)
    ad = jnp.where(diag, a, 0.0)
    e = a - ad
    a2 = mm(ad, ad)
    a4 = mm(a2, a2)
    a8 = mm(a4, a4)
    p = -ad
    for sq in (a2, a4, a8):
        p = p + sq + mm(p, sq)
    m = e + mm(p, e)
    m2 = mm(m, m)
    q = m2 - m - mm(m, m2)
    return q + p + mm(q, p)


def _gdn_kernel(act_ref, gate_ref, ba_ref, alog_ref, dtb_ref, onorm_ref, ltri_ref, y_ref,
                state_ref, u_ref, w_ref, attn_ref, qg_ref, kdt_ref, el_ref, *, ts, width, dh):
    i = pl.program_id(1)
    nh = width // dh
    nb = act_ref.shape[0]

    @pl.when(i == 0)
    def _():
        state_ref[...] = jnp.zeros(state_ref.shape, F32)

    pc = GDN_PACK * CHUNK
    ii = lax.broadcasted_iota(I32, (CHUNK, pc), 0)
    lane_c = lax.broadcasted_iota(I32, (CHUNK, pc), 1)
    jj = lane_c - (lane_c // CHUNK) * CHUNK
    blk_row = lax.broadcasted_iota(I32, (pc, 1), 0) // CHUNK
    mask_w = blk_row == lax.broadcasted_iota(I32, (1, GDN_PACK * dh), 1) // dh
    mask_c = blk_row == lax.broadcasted_iota(I32, (1, pc), 1) // CHUNK
    ltri = ltri_ref[...]

    def prep_body(cp, carry, bi):
        r0 = pl.multiple_of(cp * (GDN_GROUP * CHUNK), GDN_GROUP * CHUNK)
        ba = ba_ref[bi, pl.ds(r0, GDN_GROUP * CHUNK), :]
        beta = _sigmoid(ba)
        xg = ba + dtb_ref[...]
        softplus = jnp.maximum(xg, 0.0) + jnp.log1p(jnp.exp(-jnp.abs(xg)))
        g = -jnp.exp(alog_ref[...]) * softplus
        g1 = g.astype(BF16)
        r1 = g - g1.astype(F32)
        g2 = r1.astype(BF16)
        g3 = (r1 - g2.astype(F32)).astype(BF16)
        gc = _dot(ltri, g1) + _dot(ltri, g2) + _dot(ltri, g3)
        gc_t = gc.T
        egc = jnp.exp(gc)

        pairs = [(c, h) for c in range(GDN_GROUP) for h in range(nh)]
        rows_of = lambda c: slice(c * CHUNK, (c + 1) * CHUNK)
        col = lambda arr, lane: jnp.stack([arr[rows_of(c), lane(h):lane(h) + 1] for c, h in pairs])
        beta_t = beta.T
        egc_c = col(egc, lambda h: nh + h)
        act = lambda part: jnp.stack([act_ref[bi, pl.ds(r0 + c * CHUNK, CHUNK),
                                              part * width + h * dh:part * width + (h + 1) * dh] for c, h in pairs])
        q, k, v = act(0), act(1), act(2)

        groups = [(c, hg) for c in range(GDN_GROUP) for hg in range(nh // GDN_PACK)]
        heads_of = lambda c, hg: [c * nh + hg * GDN_PACK + j for j in range(GDN_PACK)]
        pack = lambda x: jnp.stack([jnp.concatenate([x[n] for n in heads_of(c, hg)], axis=1) for c, hg in groups])
        bd = lambda x, mask: jnp.where(mask, jnp.concatenate([x] * GDN_PACK, axis=1), 0.0).astype(BF16)
        col_p = lambda arr, lane0: jnp.stack([jnp.concatenate(
            [jnp.broadcast_to(arr[rows_of(c), lane0 + hg * GDN_PACK + j:lane0 + hg * GDN_PACK + j + 1], (CHUNK, CHUNK))
             for j in range(GDN_PACK)], axis=1) for c, hg in groups])
        row_p = lambda arr_t, lane0: jnp.stack([jnp.concatenate(
            [arr_t[lane0 + hg * GDN_PACK + j:lane0 + hg * GDN_PACK + j + 1, rows_of(c)]
             for j in range(GDN_PACK)], axis=1) for c, hg in groups])
        gc_rp = row_p(gc_t, nh)
        dec = jnp.exp(jnp.where(ii >= jj, col_p(gc, nh) - gc_rp, 0.0))
        k_bd = bd(pack(k), mask_w)
        k_p = pack(k).astype(BF16)
        a = _bmm_nt(k_p, k_bd) * (col_p(beta, 0) * jnp.where(ii > jj, dec, 0.0))
        attn = _bmm_nt(pack(q).astype(BF16), k_bd) * jnp.where(ii >= jj, dec, 0.0)
        mm = lambda x, y: _bmm(x.astype(BF16), bd(y, mask_c))
        t_full = _unit_lower_inverse_minus_eye(a, ii, jj, mm) + jnp.where(ii == jj, 1.0, 0.0).astype(F32)
        t_beta = t_full * row_p(beta_t, 0)
        u_p = _bmm(t_beta.astype(BF16), bd(pack(v), mask_w))
        w_p = _bmm((t_beta * jnp.exp(gc_rp)).astype(BF16), k_bd)
        unpack = lambda x, wd: jnp.stack([x[m][:, j * wd:(j + 1) * wd]
                                          for m in range(len(groups)) for j in range(GDN_PACK)])
        gc_r = jnp.stack([gc_t[nh + h:nh + h + 1, rows_of(c)] for c, h in pairs])
        g_last = gc_r[:, :, CHUNK - 1:CHUNK]
        results = ((u_ref, unpack(u_p, dh)), (w_ref, unpack(w_p, dh).astype(BF16)),
                   (attn_ref, unpack(attn, CHUNK).astype(BF16)), (qg_ref, (q * egc_c).astype(BF16)),
                   (kdt_ref, (jnp.swapaxes(k, 1, 2) * jnp.exp(g_last - gc_r)).astype(BF16)),
                   (el_ref, jnp.broadcast_to(jnp.exp(g_last), (GDN_GROUP * nh, 1, dh))))
        for c in range(GDN_GROUP):
            n0 = pl.multiple_of(((cp * GDN_GROUP + c) * nb + bi) * nh, nh)
            for ref, val in results:
                ref[pl.ds(n0, nh)] = val[c * nh:(c + 1) * nh]
        return carry

    for bi in range(nb):
        lax.fori_loop(jnp.int32(0), jnp.int32(ts // (GDN_GROUP * CHUNK)), functools.partial(prep_body, bi=bi), 0)

    def chunk_body(c, carry):
        r0 = pl.multiple_of(c * CHUNK, CHUNK)
        sl = pl.ds(pl.multiple_of(c * (nb * nh), nb * nh), nb * nh)
        st = state_ref[...]
        st_bf = st.astype(BF16)
        v_new = u_ref[sl] - _bmm(w_ref[sl], st_bf)
        vn_bf = v_new.astype(BF16)
        o = _bmm(qg_ref[sl], st_bf) + _bmm(attn_ref[sl], vn_bf)
        state_ref[...] = st * el_ref[sl] + _bmm(kdt_ref[sl], vn_bf)
        on = o * lax.rsqrt(jnp.mean(o * o, axis=-1, keepdims=True) + RMS_EPS) * onorm_ref[...]
        for bi in range(nb):
            for h in range(nh):
                hc = slice(h * dh, (h + 1) * dh)
                gate = _silu(gate_ref[bi, pl.ds(r0, CHUNK), hc])
                y_ref[bi, pl.ds(r0, CHUNK), hc] = (on[bi * nh + h] * gate).astype(BF16)
        return carry

    lax.fori_loop(jnp.int32(0), jnp.int32(ts // CHUNK), chunk_body, 0)


def _gdn(act, gate, ba, a_log, dt_bias, out_norm):
    b, s, w3 = act.shape
    width = w3 // 3
    dh = width // GDN_HEADS
    nb = GDN_SEQS if b % GDN_SEQS == 0 else 1
    ts = min(ROW_TILE // nb, s)
    pad_a = lambda v: jnp.pad(v.reshape(1, GDN_HEADS), ((0, 0), (GDN_HEADS, LANES - 2 * GDN_HEADS)))
    gc_rows = GDN_GROUP * CHUNK
    ri = lax.broadcasted_iota(I32, (gc_rows, gc_rows), 0)
    ci = lax.broadcasted_iota(I32, (gc_rows, gc_rows), 1)
    ltri = ((ri >= ci) & (ri // CHUNK == ci // CHUNK)).astype(BF16)
    n_mat = (ts // CHUNK) * nb * GDN_HEADS
    kern = functools.partial(_gdn_kernel, ts=ts, width=width, dh=dh)
    row = lambda c: pl.BlockSpec((nb, ts, c), lambda i, j: (i, j, 0))
    full = lambda shape: pl.BlockSpec(shape, lambda i, j: (0,) * len(shape))
    return pl.pallas_call(
        kern,
        grid=(b // nb, s // ts),
        in_specs=[row(w3), row(width), row(LANES), full((1, LANES)), full((1, LANES)),
                  full((1, dh)), full((gc_rows, gc_rows))],
        out_specs=row(width),
        out_shape=jax.ShapeDtypeStruct((b, s, width), BF16),
        scratch_shapes=[pltpu.VMEM((nb * GDN_HEADS, dh, dh), F32),
                        pltpu.VMEM((n_mat, CHUNK, dh), F32),
                        pltpu.VMEM((n_mat, CHUNK, dh), BF16),
                        pltpu.VMEM((n_mat, CHUNK, CHUNK), BF16),
                        pltpu.VMEM((n_mat, CHUNK, dh), BF16),
                        pltpu.VMEM((n_mat, dh, CHUNK), BF16),
                        pltpu.VMEM((n_mat, 1, dh), F32)],
        compiler_params=_params("arbitrary", "arbitrary"),
    )(act, gate, ba, pad_a(a_log), pad_a(dt_bias), out_norm.reshape(1, dh), ltri)


def kernel(x, c, e_mod_w, e_mod_b, e_w_in, e_pool_w, e_pool_scale, e_kv_norm, e_w_uk, e_w_uv, e_w_out, e_ln_g, e_ln_b, o_mod_w, o_mod_b, o_w_in, o_conv_w, o_a_log, o_dt_bias, o_out_norm, o_w_out, o_ln_g, o_ln_b, f_mod_w, f_mod_b, f_w_up, f_conv_w, f_conv_b, f_w_down, f_ln_g, f_ln_b):
    depth = f_mod_w.shape[0]
    alpha = (2 * depth) ** 0.25
    for layer in range(depth):
        i = layer // 2
        if layer % 2 == 0:
            mods = _modulation(c, e_mod_w, e_mod_b, i)
            u, ql, ckv, qi, kw = _even_in_proj(x, mods, e_w_in[i], e_kv_norm[i], e_w_uk[i])
            y_pool = _pool_mixer(u, e_pool_w[i], e_pool_scale[i])
            y_dsa = _dsa_attention_t(ql, qi, kw, ckv, e_w_uv[i])
            x = _out_proj_ln(x, mods, [y_pool, y_dsa], e_w_out[i], e_ln_g[i], e_ln_b[i], alpha)
        else:
            mods = _modulation(c, o_mod_w, o_mod_b, i)
            width = o_w_out.shape[1]
            act, gate, ba = _odd_in_proj(x, mods, o_w_in[i], o_conv_w[i], width)
            y = _gdn(act, gate, ba, o_a_log[i], o_dt_bias[i], o_out_norm[i])
            x = _out_proj_ln(x, mods, [y], o_w_out[i], o_ln_g[i], o_ln_b[i], alpha)
        mods = _modulation(c, f_mod_w, f_mod_b, layer)
        act = _ffn_up(x, mods, f_w_up[layer], f_conv_w[layer], f_conv_b[layer])
        x = _out_proj_ln(x, mods, [act], f_w_down[layer], f_ln_g[layer], f_ln_b[layer], alpha)
    return x
```
